```python
import math
import jax, jax.numpy as jnp
from jax import lax
import numpy as np

D_MODEL = 2048
BATCH = 16
SEQ = 2048
DEPTH = 2

N_A = DEPTH // 2
N_B = DEPTH - N_A
N_DENSE = (DEPTH + 1) // 2
N_MOE = DEPTH // 2

SSM_EXPAND = 2
D_INNER = SSM_EXPAND * D_MODEL
SSM_HEAD_DIM = 64
SSM_HEADS = D_INNER // SSM_HEAD_DIM
SSM_GROUPS = 8
SSM_HEADS_PER_GROUP = SSM_HEADS // SSM_GROUPS
SSM_STATE = 128
CONV_K = 4
CONV_DIM = D_INNER + 2 * SSM_GROUPS * SSM_STATE
D_IN_PROJ = D_INNER + CONV_DIM + SSM_HEADS
SSM_CHUNK = 128
DT_MIN = 0.001
DT_MAX = 0.1

MLA_HEADS = 16
Q_RANK = 512
KV_RANK = 512
NOPE_DIM = 128
ROPE_DIM = 64
V_DIM = 128
ROPE_THETA = 10000.0
ATTN_BLOCK = 128

D_FF = 5632
N_EXPERTS = 8
TOP_K = 2
D_FF_EXPERT = 2816

PLE_DIM = 256
DEEPNORM_ALPHA = (2.0 * DEPTH) ** 0.25
DEEPNORM_BETA = (8.0 * DEPTH) ** -0.25
LN_EPS = 1e-5
RMS_EPS = 1e-6

kernel_name = "yoco_mamba2_mla_moe_deepnorm_ple"


def layer_norm(x, g, b):
    x32 = x.astype(jnp.float32)
    mu = jnp.mean(x32, -1, keepdims=True)
    var = jnp.mean(jnp.square(x32 - mu), -1, keepdims=True)
    return ((x32 - mu) * lax.rsqrt(var + LN_EPS) * g + b).astype(x.dtype)


def rms_norm(x, g):
    x32 = x.astype(jnp.float32)
    return (x32 * lax.rsqrt(jnp.mean(jnp.square(x32), -1, keepdims=True) + RMS_EPS) * g).astype(x.dtype)


def rope_tables(positions):
    inv_freq = ROPE_THETA ** (-jnp.arange(0, ROPE_DIM, 2, dtype=jnp.float32) / ROPE_DIM)
    ang = positions.astype(jnp.float32)[..., None] * inv_freq
    return jnp.cos(ang), jnp.sin(ang)


def apply_rope(x, cos, sin):
    x1, x2 = jnp.split(x.astype(jnp.float32), 2, axis=-1)
    return jnp.concatenate([x1 * cos - x2 * sin, x2 * cos + x1 * sin], -1).astype(x.dtype)


def causal_depthwise_conv(u, w, b):
    out = lax.conv_general_dilated(
        u, w[:, None, :].astype(u.dtype), window_strides=(1,), padding=[(CONV_K - 1, 0)],
        dimension_numbers=("NWC", "WIO", "NWC"), feature_group_count=u.shape[-1])
    return out + b


def ssd_chunked_scan(xdt, a, bm, cm):
    bsz, seq = xdt.shape[:2]
    nc = seq // SSM_CHUNK

    def to_chunks(t):
        t = t.astype(jnp.float32).reshape((bsz, nc, SSM_CHUNK) + t.shape[2:])
        return jnp.moveaxis(t, 1, 0)

    xs = (to_chunks(xdt), to_chunks(a), to_chunks(bm), to_chunks(cm))
    causal = np.tril(np.ones((SSM_CHUNK, SSM_CHUNK), dtype=bool))[None, :, :, None, None]

    def step(state, inp):
        xc, ac, bc, cc = inp
        acum = jnp.cumsum(ac, axis=1)
        seg = acum[:, :, None] - acum[:, None, :]
        decay = jnp.exp(jnp.where(causal, seg, -jnp.inf))
        cb = jnp.einsum('btgn,bsgn->btsg', cc, bc)
        y = jnp.einsum('btsg,btsgh,bsghp->btghp', cb, decay, xc)
        y = y + jnp.einsum('btgn,bghpn->btghp', cc, state) * jnp.exp(acum)[..., None]
        to_end = jnp.exp(acum[:, -1:] - acum)
        state = (state * jnp.exp(acum[:, -1])[..., None, None]
                 + jnp.einsum('bsgn,bsgh,bsghp->bghpn', bc, to_end, xc))
        return state, y

    state0 = jnp.zeros((bsz, SSM_GROUPS, SSM_HEADS_PER_GROUP, SSM_HEAD_DIM, SSM_STATE), jnp.float32)
    _, ys = lax.scan(step, state0, xs)
    return jnp.moveaxis(ys, 0, 1).reshape(xdt.shape)


def mamba2_mixer(h, w_in, conv_w, conv_b, dt_bias, a_log, d_skip, norm_g, w_out):
    bsz, seq, _ = h.shape
    zxbcdt = h @ w_in
    z, xbc, dt = jnp.split(zxbcdt, [D_INNER, D_INNER + CONV_DIM], axis=-1)
    xbc = jax.nn.silu(causal_depthwise_conv(xbc, conv_w, conv_b))
    xs, bm, cm = jnp.split(xbc, [D_INNER, D_INNER + SSM_GROUPS * SSM_STATE], axis=-1)
    xs = xs.reshape(bsz, seq, SSM_GROUPS, SSM_HEADS_PER_GROUP, SSM_HEAD_DIM).astype(jnp.float32)
    bm = bm.reshape(bsz, seq, SSM_GROUPS, SSM_STATE)
    cm = cm.reshape(bsz, seq, SSM_GROUPS, SSM_STATE)
    dt = jax.nn.softplus(dt.astype(jnp.float32) + dt_bias.astype(jnp.float32))
    dt = dt.reshape(bsz, seq, SSM_GROUPS, SSM_HEADS_PER_GROUP)
    a = -jnp.exp(a_log.astype(jnp.float32)).reshape(SSM_GROUPS, SSM_HEADS_PER_GROUP)
    y = ssd_chunked_scan(xs * dt[..., None], a * dt, bm, cm)
    y = y + d_skip.astype(jnp.float32).reshape(SSM_GROUPS, SSM_HEADS_PER_GROUP)[..., None] * xs
    gsz = D_INNER // SSM_GROUPS
    y = y.reshape(bsz, seq, SSM_GROUPS, gsz) * jax.nn.silu(z.astype(jnp.float32)).reshape(bsz, seq, SSM_GROUPS, gsz)
    y = rms_norm(y, norm_g.reshape(SSM_GROUPS, gsz)).reshape(bsz, seq, D_INNER)
    return y.astype(h.dtype) @ w_out


def shared_latent_kv(h, w_down, norm_g, w_rope, w_uk, w_uv, cos, sin):
    bsz, seq, _ = h.shape
    c_kv = rms_norm(h @ w_down, norm_g)
    k_nope = (c_kv @ w_uk).reshape(bsz, seq, MLA_HEADS, NOPE_DIM)
    v = (c_kv @ w_uv).reshape(bsz, seq, MLA_HEADS, V_DIM)
    k_rope = apply_rope(h @ w_rope, cos, sin)
    return k_nope, k_rope, v


def mla_attention(h, k_nope, k_rope, v, w_dq, q_norm_g, w_uq, w_o, cos, sin):
    bsz, seq, _ = h.shape
    q = (rms_norm(h @ w_dq, q_norm_g) @ w_uq).reshape(bsz, seq, MLA_HEADS, NOPE_DIM + ROPE_DIM)
    q_nope, q_rope = jnp.split(q, [NOPE_DIM], axis=-1)
    q_rope = apply_rope(q_rope, cos[:, :, None], sin[:, :, None])
    scale = (NOPE_DIM + ROPE_DIM) ** -0.5
    outs = []
    for start in range(0, seq, ATTN_BLOCK):
        end = start + ATTN_BLOCK
        s = (jnp.einsum('bqhd,bkhd->bhqk', q_nope[:, start:end], k_nope[:, :end])
             + jnp.einsum('bqhr,bkr->bhqk', q_rope[:, start:end], k_rope[:, :end])).astype(jnp.float32) * scale
        mask = (start + np.arange(ATTN_BLOCK))[:, None] >= np.arange(end)[None, :]
        probs = jax.nn.softmax(jnp.where(mask, s, -jnp.inf), axis=-1).astype(v.dtype)
        outs.append(jnp.einsum('bhqk,bkhd->bqhd', probs, v[:, :end]))
    o = jnp.concatenate(outs, axis=1).reshape(bsz, seq, MLA_HEADS * V_DIM)
    return o @ w_o


def swiglu(h, w_gate, w_up, w_down):
    return (jax.nn.silu(h @ w_gate) * (h @ w_up)) @ w_down


def moe_swiglu(h, w_router, b_router, w_gate, w_up, w_down):
    logits = (h @ w_router).astype(jnp.float32) + b_router.astype(jnp.float32)
    top_val, top_idx = lax.top_k(logits, TOP_K)
    top_w = jax.nn.softmax(top_val, axis=-1)
    combine = jnp.sum(jax.nn.one_hot(top_idx, N_EXPERTS, dtype=jnp.float32) * top_w[..., None], axis=-2)
    out = jnp.zeros_like(h)
    for e in range(N_EXPERTS):
        out = out + combine[..., e:e + 1].astype(h.dtype) * swiglu(h, w_gate[e], w_up[e], w_down[e])
    return out


def setup_inputs(seed: int = 0) -> dict:
    key = jax.random.key(seed)
    ks = iter(jax.random.split(key, 48))

    def normal(shape, scale):
        return jax.random.normal(next(ks), shape, jnp.float32) * scale

    def gain(shape):
        return 1.0 + normal(shape, 0.02)

    x = normal((BATCH, SEQ, D_MODEL), 1.0)
    p = normal((DEPTH, BATCH, SEQ, PLE_DIM), 1.0)
    start = jax.random.randint(next(ks), (BATCH, 1), 0, 4096, dtype=jnp.int32)
    positions = start + jnp.arange(SEQ, dtype=jnp.int32)[None, :]

    ssm_w_in = normal((N_A, D_MODEL, D_IN_PROJ), D_MODEL ** -0.5)
    ssm_conv_w = normal((N_A, CONV_K, CONV_DIM), CONV_K ** -0.5)
    ssm_conv_b = normal((N_A, CONV_DIM), 0.02)
    dt0 = jnp.exp(jax.random.uniform(next(ks), (N_A, SSM_HEADS), jnp.float32,
                                     math.log(DT_MIN), math.log(DT_MAX)))
    ssm_dt_bias = dt0 + jnp.log(-jnp.expm1(-dt0))
    ssm_a_log = jnp.log(jax.random.uniform(next(ks), (N_A, SSM_HEADS), jnp.float32, 1.0, 16.0))
    ssm_d = gain((N_A, SSM_HEADS))
    ssm_norm_g = gain((N_A, D_INNER))
    ssm_w_out = normal((N_A, D_INNER, D_MODEL), D_INNER ** -0.5 * DEEPNORM_BETA)

    kv_w_down = normal((D_MODEL, KV_RANK), D_MODEL ** -0.5)
    kv_norm_g = gain((KV_RANK,))
    kv_w_rope = normal((D_MODEL, ROPE_DIM), D_MODEL ** -0.5)
    kv_w_uk = normal((KV_RANK, MLA_HEADS * NOPE_DIM), KV_RANK ** -0.5)
    kv_w_uv = normal((KV_RANK, MLA_HEADS * V_DIM), KV_RANK ** -0.5)

    mla_w_dq = normal((N_B, D_MODEL, Q_RANK), D_MODEL ** -0.5)
    mla_q_norm_g = gain((N_B, Q_RANK))
    mla_w_uq = normal((N_B, Q_RANK, MLA_HEADS * (NOPE_DIM + ROPE_DIM)), Q_RANK ** -0.5)
    mla_w_o = normal((N_B, MLA_HEADS * V_DIM, D_MODEL), (MLA_HEADS * V_DIM) ** -0.5 * DEEPNORM_BETA)

    ffn_w_gate = normal((N_DENSE, D_MODEL, D_FF), D_MODEL ** -0.5)
    ffn_w_up = normal((N_DENSE, D_MODEL, D_FF), D_MODEL ** -0.5)
    ffn_w_down = normal((N_DENSE, D_FF, D_MODEL), D_FF ** -0.5 * DEEPNORM_BETA)

    moe_w_router = normal((N_MOE, D_MODEL, N_EXPERTS), D_MODEL ** -0.5)
    moe_b_router = normal((N_MOE, N_EXPERTS), 0.01)
    moe_w_gate = normal((N_MOE, N_EXPERTS, D_MODEL, D_FF_EXPERT), D_MODEL ** -0.5)
    moe_w_up = normal((N_MOE, N_EXPERTS, D_MODEL, D_FF_EXPERT), D_MODEL ** -0.5)
    moe_w_down = normal((N_MOE, N_EXPERTS, D_FF_EXPERT, D_MODEL), D_FF_EXPERT ** -0.5 * DEEPNORM_BETA)

    ln1_g = gain((DEPTH, D_MODEL))
    ln1_b = normal((DEPTH, D_MODEL), 0.02)
    ln2_g = gain((DEPTH, D_MODEL))
    ln2_b = normal((DEPTH, D_MODEL), 0.02)

    ple_w_proj = normal((DEPTH, PLE_DIM, D_MODEL), PLE_DIM ** -0.5)
    ple_w_gate = normal((DEPTH, D_MODEL, D_MODEL), D_MODEL ** -0.5)

    return {
        "x": x, "p": p, "positions": positions,
        "ssm_w_in": ssm_w_in, "ssm_conv_w": ssm_conv_w, "ssm_conv_b": ssm_conv_b,
        "ssm_dt_bias": ssm_dt_bias, "ssm_a_log": ssm_a_log, "ssm_d": ssm_d,
        "ssm_norm_g": ssm_norm_g, "ssm_w_out": ssm_w_out,
        "kv_w_down": kv_w_down, "kv_norm_g": kv_norm_g, "kv_w_rope": kv_w_rope,
        "kv_w_uk": kv_w_uk, "kv_w_uv": kv_w_uv,
        "mla_w_dq": mla_w_dq, "mla_q_norm_g": mla_q_norm_g, "mla_w_uq": mla_w_uq, "mla_w_o": mla_w_o,
        "ffn_w_gate": ffn_w_gate, "ffn_w_up": ffn_w_up, "ffn_w_down": ffn_w_down,
        "moe_w_router": moe_w_router, "moe_b_router": moe_b_router,
        "moe_w_gate": moe_w_gate, "moe_w_up": moe_w_up, "moe_w_down": moe_w_down,
        "ln1_g": ln1_g, "ln1_b": ln1_b, "ln2_g": ln2_g, "ln2_b": ln2_b,
        "ple_w_proj": ple_w_proj, "ple_w_gate": ple_w_gate,
    }


def reference(x, p, positions,
              ssm_w_in, ssm_conv_w, ssm_conv_b, ssm_dt_bias, ssm_a_log, ssm_d, ssm_norm_g, ssm_w_out,
              kv_w_down, kv_norm_g, kv_w_rope, kv_w_uk, kv_w_uv,
              mla_w_dq, mla_q_norm_g, mla_w_uq, mla_w_o,
              ffn_w_gate, ffn_w_up, ffn_w_down,
              moe_w_router, moe_b_router, moe_w_gate, moe_w_up, moe_w_down,
              ln1_g, ln1_b, ln2_g, ln2_b,
              ple_w_proj, ple_w_gate):
    cos, sin = rope_tables(positions)
    h = x
    k_nope = k_rope = v = None
    for i in range(DEPTH):
        if i < N_A:
            mix = mamba2_mixer(h, ssm_w_in[i], ssm_conv_w[i], ssm_conv_b[i], ssm_dt_bias[i],
                               ssm_a_log[i], ssm_d[i], ssm_norm_g[i], ssm_w_out[i])
        else:
            if i == N_A:
                k_nope, k_rope, v = shared_latent_kv(h, kv_w_down, kv_norm_g, kv_w_rope,
                                                     kv_w_uk, kv_w_uv, cos, sin)
            j = i - N_A
            mix = mla_attention(h, k_nope, k_rope, v, mla_w_dq[j], mla_q_norm_g[j],
                                mla_w_uq[j], mla_w_o[j], cos, sin)
        h = layer_norm(DEEPNORM_ALPHA * h + mix, ln1_g[i], ln1_b[i])
        j = i // 2
        if i % 2 == 0:
            ff = swiglu(h, ffn_w_gate[j], ffn_w_up[j], ffn_w_down[j])
        else:
            ff = moe_swiglu(h, moe_w_router[j], moe_b_router[j], moe_w_gate[j], moe_w_up[j], moe_w_down[j])
        h = layer_norm(DEEPNORM_ALPHA * h + ff, ln2_g[i], ln2_b[i])
        h = h + jax.nn.sigmoid(h @ ple_w_gate[i]) * (p[i] @ ple_w_proj[i])
    return h
```

```python
import functools

import jax
import jax.numpy as jnp
from jax import lax
from jax.experimental import pallas as pl
from jax.experimental.pallas import tpu as pltpu

F32 = jnp.float32
BF16 = jnp.bfloat16
HIGHEST = lax.Precision.HIGHEST

D_MODEL = 2048
BATCH = 16
SEQ = 2048
TOKENS = BATCH * SEQ
DEPTH = 2

D_INNER = 4096
SSM_HEAD_DIM = 64
SSM_HEADS = 64
SSM_GROUPS = 8
HEADS_PER_GROUP = 8
SSM_STATE = 128
GROUP_WIDTH = HEADS_PER_GROUP * SSM_HEAD_DIM
CONV_K = 4
CONV_DIM = D_INNER + 2 * SSM_GROUPS * SSM_STATE
SSM_CHUNK = 128
N_CHUNKS = SEQ // SSM_CHUNK

MLA_HEADS = 16
Q_RANK = 512
KV_RANK = 512
NOPE_DIM = 128
ROPE_DIM = 64
V_DIM = 128
ROPE_THETA = 10000.0
HEAD_PAD = 256

D_FF = 5632
N_EXPERTS = 8
D_FF_EXPERT = 2816

PLE_DIM = 256
DEEPNORM_ALPHA = (2.0 * DEPTH) ** 0.25
LN_EPS = 1e-5
RMS_EPS = 1e-6

VMEM_LIMIT = 56 * 1024 * 1024


def _params(*sem):
    return pltpu.CompilerParams(dimension_semantics=sem, vmem_limit_bytes=VMEM_LIMIT)


def _layer_norm(x, g, b):
    mu = jnp.mean(x, axis=-1, keepdims=True)
    xc = x - mu
    var = jnp.mean(xc * xc, axis=-1, keepdims=True)
    return xc * lax.rsqrt(var + LN_EPS) * g + b


def _rms_norm(x, g):
    return x * lax.rsqrt(jnp.mean(x * x, axis=-1, keepdims=True) + RMS_EPS) * g


def _silu(x):
    return x * (1.0 / (1.0 + jnp.exp(-x)))


def _sigmoid(x):
    return 1.0 / (1.0 + jnp.exp(-x))


def _rope_table_kernel(pos_ref, freq_ref, o_ref):
    ang = pos_ref[...].astype(F32) * freq_ref[...]
    lane = lax.broadcasted_iota(jnp.int32, ang.shape, 1)
    c = jnp.cos(ang)
    s = jnp.sin(ang)
    o_ref[...] = jnp.where(lane < 64, c, jnp.where(lane < 96, -s, s))


def _rope_table(pos_col, freq_row, tm=1024):
    return pl.pallas_call(
        _rope_table_kernel,
        grid=(TOKENS // tm,),
        in_specs=[pl.BlockSpec((tm, 1), lambda i: (i, 0)),
                  pl.BlockSpec((1, 128), lambda i: (0, 0))],
        out_specs=pl.BlockSpec((tm, 128), lambda i: (i, 0)),
        out_shape=jax.ShapeDtypeStruct((TOKENS, 128), F32),
        compiler_params=_params("parallel"),
        name="rope_table",
    )(pos_col, freq_row)


def _matmul_kernel(a_ref, w_ref, o_ref, a_bf):
    @pl.when(pl.program_id(1) == 0)
    def _():
        a_bf[...] = a_ref[...].astype(BF16)

    o_ref[...] = jnp.dot(a_bf[...], w_ref[...], preferred_element_type=F32).astype(o_ref.dtype)


def _matmul(a, w, out_dtype, tm, tn):
    m, k = a.shape
    n = w.shape[1]
    return pl.pallas_call(
        _matmul_kernel,
        grid=(m // tm, n // tn),
        in_specs=[pl.BlockSpec((tm, k), lambda i, j: (i, 0)),
                  pl.BlockSpec((k, tn), lambda i, j: (0, j))],
        out_specs=pl.BlockSpec((tm, tn), lambda i, j: (i, j)),
        out_shape=jax.ShapeDtypeStruct((m, n), out_dtype),
        scratch_shapes=[pltpu.VMEM((tm, k), BF16)],
        compiler_params=_params("parallel", "arbitrary"),
        name="matmul",
    )(a, w)


def _dt_kernel(x_ref, w_ref, bias_ref, alog_ref, dt_ref, acum_ref, *, tm):
    raw = jnp.dot(x_ref[...], w_ref[...], preferred_element_type=F32, precision=HIGHEST)
    raw = raw + bias_ref[...]
    dt = jnp.maximum(raw, 0.0) + jnp.log1p(jnp.exp(-jnp.abs(raw)))
    dt_ref[...] = dt
    a = -jnp.exp(alog_ref[...]) * dt
    row = lax.broadcasted_iota(jnp.int32, (SSM_CHUNK, SSM_CHUNK), 0)
    col = lax.broadcasted_iota(jnp.int32, (SSM_CHUNK, SSM_CHUNK), 1)
    tril = jnp.where(row >= col, 1.0, 0.0).astype(F32)
    for c in range(tm // SSM_CHUNK):
        sl = slice(c * SSM_CHUNK, (c + 1) * SSM_CHUNK)
        acum_ref[sl, :] = jnp.dot(tril, a[sl, :], preferred_element_type=F32, precision=HIGHEST)


def _dt_prep(x, w_dt, dt_bias, a_log, tm=512):
    return pl.pallas_call(
        functools.partial(_dt_kernel, tm=tm),
        grid=(TOKENS // tm,),
        in_specs=[pl.BlockSpec((tm, D_MODEL), lambda i: (i, 0)),
                  pl.BlockSpec((D_MODEL, SSM_HEADS), lambda i: (0, 0)),
                  pl.BlockSpec((1, SSM_HEADS), lambda i: (0, 0)),
                  pl.BlockSpec((1, SSM_HEADS), lambda i: (0, 0))],
        out_specs=[pl.BlockSpec((tm, SSM_HEADS), lambda i: (i, 0)),
                   pl.BlockSpec((tm, SSM_HEADS), lambda i: (i, 0))],
        out_shape=[jax.ShapeDtypeStruct((TOKENS, SSM_HEADS), F32),
                   jax.ShapeDtypeStruct((TOKENS, SSM_HEADS), F32)],
        compiler_params=_params("parallel"),
        name="dt_prep",
    )(x, w_dt, dt_bias, a_log)


CONV_ROWS = 256
CONV_HALO = 16


def _conv_kernel(u_ref, w_ref, b_ref, o_ref):
    w = w_ref[...]
    b = b_ref[...]
    tc = u_ref.shape[1]
    for i in range(SEQ // CONV_ROWS):
        r = i * CONV_ROWS
        cur = u_ref[r:r + CONV_ROWS, :].astype(F32)
        if i == 0:
            halo = jnp.zeros((CONV_HALO, tc), F32)
        else:
            halo = u_ref[r - CONV_HALO:r, :].astype(F32)
        win = jnp.concatenate([halo, cur], axis=0)
        acc = b + w[CONV_K - 1:CONV_K, :] * cur
        for k in range(CONV_K - 1):
            shifted = pltpu.roll(win, CONV_K - 1 - k, axis=0)[CONV_HALO:, :]
            acc = acc + w[k:k + 1, :] * shifted
        o_ref[r:r + CONV_ROWS, :] = _silu(acc).astype(o_ref.dtype)


def _conv_silu(zx, conv_w, conv_b, tc=512):
    col0 = D_INNER // tc
    return pl.pallas_call(
        _conv_kernel,
        grid=(BATCH, CONV_DIM // tc),
        in_specs=[pl.BlockSpec((SEQ, tc), lambda b, j: (b, col0 + j)),
                  pl.BlockSpec((CONV_K, tc), lambda b, j: (0, j)),
                  pl.BlockSpec((1, tc), lambda b, j: (0, j))],
        out_specs=pl.BlockSpec((SEQ, tc), lambda b, j: (b, j)),
        out_shape=jax.ShapeDtypeStruct((TOKENS, CONV_DIM), BF16),
        compiler_params=_params("parallel", "parallel"),
        name="conv_silu",
    )(zx, conv_w, conv_b)


def _ssd_kernel(x_ref, b_ref, c_ref, z_ref, dt_ref, acum_ref, acumt_ref, d_ref, g_ref,
                o_ref, state_ref):
    @pl.when(pl.program_id(2) == 0)
    def _():
        state_ref[...] = jnp.zeros_like(state_ref)

    q = SSM_CHUNK
    xf = x_ref[...].astype(F32)
    bm = b_ref[...]
    cm = c_ref[...]
    dt = dt_ref[0, 0]
    ac = acum_ref[0, 0]
    act = acumt_ref[0, 0]

    hrow = lax.broadcasted_iota(jnp.int32, (HEADS_PER_GROUP, GROUP_WIDTH), 0)
    hcol = lax.broadcasted_iota(jnp.int32, (HEADS_PER_GROUP, GROUP_WIDTH), 1)
    expand = jnp.where(hrow == hcol // SSM_HEAD_DIM, 1.0, 0.0).astype(F32)

    def widen(v):
        return jnp.dot(v, expand, preferred_element_type=F32, precision=HIGHEST)

    ac_last = ac[q - 1:q, :]
    xdt = xf * widen(dt)
    decay_out = widen(jnp.exp(ac))
    to_end = widen(jnp.exp(ac_last - ac))
    xdt_bf = xdt.astype(BF16)

    cb = lax.dot_general(cm, bm, (((1,), (1,)), ((), ())), preferred_element_type=F32)
    state = state_ref[...]
    y = jnp.dot(cm, state.astype(BF16), preferred_element_type=F32) * decay_out

    trow = lax.broadcasted_iota(jnp.int32, (q, q), 0)
    tcol = lax.broadcasted_iota(jnp.int32, (q, q), 1)
    causal = trow >= tcol
    lane = lax.broadcasted_iota(jnp.int32, (q, 2 * SSM_HEAD_DIM), 1)
    sel_row = lax.broadcasted_iota(jnp.int32, (HEADS_PER_GROUP, q), 0)
    pairs = []
    for pr in range(HEADS_PER_GROUP // 2):
        xpair = xdt_bf[:, pr * 128:(pr + 1) * 128]
        halves = []
        for j in (2 * pr, 2 * pr + 1):
            onehot = jnp.where(sel_row == j, 1.0, 0.0).astype(F32)
            colv = jnp.dot(ac, onehot, preferred_element_type=F32, precision=HIGHEST)
            seg = jnp.where(causal, colv - act[j:j + 1, :], -jnp.inf)
            m = (cb * jnp.exp(seg)).astype(BF16)
            halves.append(jnp.dot(m, xpair, preferred_element_type=F32))
        pairs.append(jnp.where(lane < SSM_HEAD_DIM, halves[0], halves[1]))
    y = y + jnp.concatenate(pairs, axis=1)

    upd = lax.dot_general(bm, (xdt * to_end).astype(BF16), (((0,), (0,)), ((), ())),
                          preferred_element_type=F32)
    state_ref[...] = state * decay_out[q - 1:q, :] + upd

    y = y + d_ref[...] * xf
    y = y * _silu(z_ref[...].astype(F32))
    o_ref[...] = _rms_norm(y, g_ref[...]).astype(o_ref.dtype)


def _ssd(xbc, zx, dt4, acum4, acumt4, d_full, norm_g):
    q = SSM_CHUNK
    nb = SSM_GROUPS * SSM_STATE // q
    x_cols = D_INNER // q
    row = lambda b, g, c: b * N_CHUNKS + c
    return pl.pallas_call(
        _ssd_kernel,
        grid=(BATCH, SSM_GROUPS, N_CHUNKS),
        in_specs=[
            pl.BlockSpec((q, GROUP_WIDTH), lambda b, g, c: (row(b, g, c), g)),
            pl.BlockSpec((q, SSM_STATE), lambda b, g, c: (row(b, g, c), x_cols + g)),
            pl.BlockSpec((q, SSM_STATE), lambda b, g, c: (row(b, g, c), x_cols + nb + g)),
            pl.BlockSpec((q, GROUP_WIDTH), lambda b, g, c: (row(b, g, c), g)),
            pl.BlockSpec((1, 1, q, HEADS_PER_GROUP), lambda b, g, c: (b, g, c, 0)),
            pl.BlockSpec((1, 1, q, HEADS_PER_GROUP), lambda b, g, c: (b, g, c, 0)),
            pl.BlockSpec((1, 1, HEADS_PER_GROUP, q), lambda b, g, c: (b, g, 0, c)),
            pl.BlockSpec((1, GROUP_WIDTH), lambda b, g, c: (0, g)),
            pl.BlockSpec((1, GROUP_WIDTH), lambda b, g, c: (0, g)),
        ],
        out_specs=pl.BlockSpec((q, GROUP_WIDTH), lambda b, g, c: (row(b, g, c), g)),
        out_shape=jax.ShapeDtypeStruct((TOKENS, D_INNER), BF16),
        scratch_shapes=[pltpu.VMEM((SSM_STATE, GROUP_WIDTH), F32)],
        compiler_params=_params("parallel", "parallel", "arbitrary"),
        name="ssd_scan",
    )(xbc, xbc, xbc, zx, dt4, acum4, acumt4, d_full, norm_g)


def _matmul_ln_kernel(a_ref, w_ref, res_ref, g_ref, b_ref, o_ref, acc_ref):
    k = pl.program_id(1)

    @pl.when(k == 0)
    def _():
        acc_ref[...] = jnp.zeros_like(acc_ref)

    acc_ref[...] += jnp.dot(a_ref[...], w_ref[...], preferred_element_type=F32)

    @pl.when(k == pl.num_programs(1) - 1)
    def _():
        x = DEEPNORM_ALPHA * res_ref[...] + acc_ref[...]
        o_ref[...] = _layer_norm(x, g_ref[...], b_ref[...])


def _matmul_ln(a, w, res, g, b, tm=512, tk=1024):
    m, k = a.shape
    n = w.shape[1]
    return pl.pallas_call(
        _matmul_ln_kernel,
        grid=(m // tm, k // tk),
        in_specs=[pl.BlockSpec((tm, tk), lambda i, kk: (i, kk)),
                  pl.BlockSpec((tk, n), lambda i, kk: (kk, 0)),
                  pl.BlockSpec((tm, n), lambda i, kk: (i, 0)),
                  pl.BlockSpec((1, n), lambda i, kk: (0, 0)),
                  pl.BlockSpec((1, n), lambda i, kk: (0, 0))],
        out_specs=pl.BlockSpec((tm, n), lambda i, kk: (i, 0)),
        out_shape=jax.ShapeDtypeStruct((m, n), F32),
        scratch_shapes=[pltpu.VMEM((tm, n), F32)],
        compiler_params=_params("parallel", "arbitrary"),
        name="matmul_ln",
    )(a, w, res, g, b)


def _swiglu_ln_kernel(h_ref, wg_ref, wu_ref, wd_ref, g_ref, b_ref, o_ref, h_bf, acc_ref):
    f = pl.program_id(1)

    @pl.when(f == 0)
    def _():
        h_bf[...] = h_ref[...].astype(BF16)
        acc_ref[...] = jnp.zeros_like(acc_ref)

    a = h_bf[...]
    gate = jnp.dot(a, wg_ref[...], preferred_element_type=F32)
    up = jnp.dot(a, wu_ref[...], preferred_element_type=F32)
    mid = (_silu(gate) * up).astype(BF16)
    acc_ref[...] += jnp.dot(mid, wd_ref[...], preferred_element_type=F32)

    @pl.when(f == pl.num_programs(1) - 1)
    def _():
        x = DEEPNORM_ALPHA * h_ref[...] + acc_ref[...]
        o_ref[...] = _layer_norm(x, g_ref[...], b_ref[...])


def _swiglu_ln(h, wg, wu, wd, g, b, tm=512, tf=512):
    m = h.shape[0]
    ff = wg.shape[1]
    return pl.pallas_call(
        _swiglu_ln_kernel,
        grid=(m // tm, ff // tf),
        in_specs=[pl.BlockSpec((tm, D_MODEL), lambda i, f: (i, 0)),
                  pl.BlockSpec((D_MODEL, tf), lambda i, f: (0, f)),
                  pl.BlockSpec((D_MODEL, tf), lambda i, f: (0, f)),
                  pl.BlockSpec((tf, D_MODEL), lambda i, f: (f, 0)),
                  pl.BlockSpec((1, D_MODEL), lambda i, f: (0, 0)),
                  pl.BlockSpec((1, D_MODEL), lambda i, f: (0, 0))],
        out_specs=pl.BlockSpec((tm, D_MODEL), lambda i, f: (i, 0)),
        out_shape=jax.ShapeDtypeStruct((m, D_MODEL), F32),
        scratch_shapes=[pltpu.VMEM((tm, D_MODEL), BF16), pltpu.VMEM((tm, D_MODEL), F32)],
        compiler_params=_params("parallel", "arbitrary"),
        name="swiglu_ln",
    )(h, wg, wu, wd, g, b)


def _ple_kernel(h_ref, p_ref, wg_ref, wp_ref, o_ref, h_bf, *, tn):
    j = pl.program_id(1)

    @pl.when(j == 0)
    def _():
        h_bf[...] = h_ref[...].astype(BF16)

    gate = jnp.dot(h_bf[...], wg_ref[...], preferred_element_type=F32)
    proj = jnp.dot(p_ref[...].astype(BF16), wp_ref[...], preferred_element_type=F32)
    res = h_ref[:, pl.ds(pl.multiple_of(j * tn, tn), tn)]
    o_ref[...] = res + _sigmoid(gate) * proj


def _ple(h, p, wg, wp, tm=512, tn=1024):
    m = h.shape[0]
    return pl.pallas_call(
        functools.partial(_ple_kernel, tn=tn),
        grid=(m // tm, D_MODEL // tn),
        in_specs=[pl.BlockSpec((tm, D_MODEL), lambda i, j: (i, 0)),
                  pl.BlockSpec((tm, PLE_DIM), lambda i, j: (i, 0)),
                  pl.BlockSpec((D_MODEL, tn), lambda i, j: (0, j)),
                  pl.BlockSpec((PLE_DIM, tn), lambda i, j: (0, j))],
        out_specs=pl.BlockSpec((tm, tn), lambda i, j: (i, j)),
        out_shape=jax.ShapeDtypeStruct((m, D_MODEL), F32),
        scratch_shapes=[pltpu.VMEM((tm, D_MODEL), BF16)],
        compiler_params=_params("parallel", "arbitrary"),
        name="ple",
    )(h, p, wg, wp)


def _latent_kernel(h_ref, w_ref, gkv_ref, gq_ref, tab_ref, ckv_ref, cq_ref, kr_ref):
    r = jnp.dot(h_ref[...].astype(BF16), w_ref[...], preferred_element_type=F32)
    ckv_ref[...] = _rms_norm(r[:, :KV_RANK], gkv_ref[...]).astype(BF16)
    cq_ref[...] = _rms_norm(r[:, KV_RANK:KV_RANK + Q_RANK], gq_ref[...]).astype(BF16)
    t = r[:, KV_RANK + Q_RANK:] * tab_ref[...]
    rot = t + pltpu.roll(t, ROPE_DIM, axis=1)
    lane = lax.broadcasted_iota(jnp.int32, rot.shape, 1)
    kr_ref[...] = jnp.where(lane < ROPE_DIM, rot, 0.0).astype(BF16)


def _latent(h, w_cat, g_kv, g_q, table, tm=512):
    n = w_cat.shape[1]
    return pl.pallas_call(
        _latent_kernel,
        grid=(TOKENS // tm,),
        in_specs=[pl.BlockSpec((tm, D_MODEL), lambda i: (i, 0)),
                  pl.BlockSpec((D_MODEL, n), lambda i: (0, 0)),
                  pl.BlockSpec((1, KV_RANK), lambda i: (0, 0)),
                  pl.BlockSpec((1, Q_RANK), lambda i: (0, 0)),
                  pl.BlockSpec((tm, 128), lambda i: (i, 0))],
        out_specs=[pl.BlockSpec((tm, KV_RANK), lambda i: (i, 0)),
                   pl.BlockSpec((tm, Q_RANK), lambda i: (i, 0)),
                   pl.BlockSpec((tm, 128), lambda i: (i, 0))],
        out_shape=[jax.ShapeDtypeStruct((TOKENS, KV_RANK), BF16),
                   jax.ShapeDtypeStruct((TOKENS, Q_RANK), BF16),
                   jax.ShapeDtypeStruct((TOKENS, 128), BF16)],
        compiler_params=_params("parallel"),
        name="latent",
    )(h, w_cat, g_kv, g_q, table)


def _kv_up_kernel(c_ref, wk_ref, wv_ref, kr_ref, k_ref, v_ref):
    c = c_ref[...]
    kn = jnp.dot(c, wk_ref[...], preferred_element_type=F32).astype(BF16)
    v_ref[...] = jnp.dot(c, wv_ref[...], preferred_element_type=F32).astype(BF16)
    kr = kr_ref[...]
    for h in range(MLA_HEADS):
        k_ref[:, h * HEAD_PAD:h * HEAD_PAD + NOPE_DIM] = kn[:, h * NOPE_DIM:(h + 1) * NOPE_DIM]
        k_ref[:, h * HEAD_PAD + NOPE_DIM:(h + 1) * HEAD_PAD] = kr


def _kv_up(c_kv, w_uk, w_uv, k_rope, tm=512):
    return pl.pallas_call(
        _kv_up_kernel,
        grid=(TOKENS // tm,),
        in_specs=[pl.BlockSpec((tm, KV_RANK), lambda i: (i, 0)),
                  pl.BlockSpec((KV_RANK, MLA_HEADS * NOPE_DIM), lambda i: (0, 0)),
                  pl.BlockSpec((KV_RANK, MLA_HEADS * V_DIM), lambda i: (0, 0)),
                  pl.BlockSpec((tm, 128), lambda i: (i, 0))],
        out_specs=[pl.BlockSpec((tm, MLA_HEADS * HEAD_PAD), lambda i: (i, 0)),
                   pl.BlockSpec((tm, MLA_HEADS * V_DIM), lambda i: (i, 0))],
        out_shape=[jax.ShapeDtypeStruct((TOKENS, MLA_HEADS * HEAD_PAD), BF16),
                   jax.ShapeDtypeStruct((TOKENS, MLA_HEADS * V_DIM), BF16)],
        compiler_params=_params("parallel"),
        name="kv_up",
    )(c_kv, w_uk, w_uv, k_rope)


def _q_up_kernel(c_ref, w_ref, tab_ref, q_ref):
    scale = (NOPE_DIM + ROPE_DIM) ** -0.5
    r = jnp.dot(c_ref[...], w_ref[...], preferred_element_type=F32)
    tab = tab_ref[...]
    for h in range(MLA_HEADS):
        base = h * HEAD_PAD
        q_ref[:, base:base + NOPE_DIM] = (r[:, base:base + NOPE_DIM] * scale).astype(BF16)
        t = r[:, base + NOPE_DIM:base + HEAD_PAD] * tab
        rot = (t + pltpu.roll(t, ROPE_DIM, axis=1)) * scale
        q_ref[:, base + NOPE_DIM:base + HEAD_PAD] = rot.astype(BF16)


def _q_up(c_q, w_q, table, tm=512):
    return pl.pallas_call(
        _q_up_kernel,
        grid=(TOKENS // tm,),
        in_specs=[pl.BlockSpec((tm, Q_RANK), lambda i: (i, 0)),
                  pl.BlockSpec((Q_RANK, MLA_HEADS * HEAD_PAD), lambda i: (0, 0)),
                  pl.BlockSpec((tm, 128), lambda i: (i, 0))],
        out_specs=pl.BlockSpec((tm, MLA_HEADS * HEAD_PAD), lambda i: (i, 0)),
        out_shape=jax.ShapeDtypeStruct((TOKENS, MLA_HEADS * HEAD_PAD), BF16),
        compiler_params=_params("parallel"),
        name="q_up",
    )(c_q, w_q, table)


ATT_TQ = 256
ATT_TK = 256


def _attn_kernel(q_ref, k_ref, v_ref, o_ref):
    qi = pl.program_id(2)
    q = q_ref[...]

    def scores(j):
        k = k_ref[pl.ds(pl.multiple_of(j * ATT_TK, ATT_TK), ATT_TK), :]
        return lax.dot_general(q, k, (((1,), (1,)), ((), ())), preferred_element_type=F32)

    def update(j, s, carry):
        m, l, acc = carry
        m_new = jnp.maximum(m, jnp.max(s, axis=-1, keepdims=True))
        alpha = jnp.exp(m - m_new)
        p = jnp.exp(s - m_new)
        l = alpha * l + jnp.sum(p, axis=-1, keepdims=True)
        v = v_ref[pl.ds(pl.multiple_of(j * ATT_TK, ATT_TK), ATT_TK), :]
        acc = alpha * acc + jnp.dot(p.astype(BF16), v, preferred_element_type=F32)
        return m_new, l, acc

    def body(j, carry):
        return update(j, scores(j), carry)

    init = (jnp.full((ATT_TQ, 1), -jnp.inf, F32), jnp.zeros((ATT_TQ, 1), F32),
            jnp.zeros((ATT_TQ, V_DIM), F32))
    carry = lax.fori_loop(0, qi, body, init)
    row = lax.broadcasted_iota(jnp.int32, (ATT_TQ, ATT_TK), 0)
    col = lax.broadcasted_iota(jnp.int32, (ATT_TQ, ATT_TK), 1)
    s = jnp.where(row >= col, scores(qi), -jnp.inf)
    _, l, acc = update(qi, s, carry)
    o_ref[...] = (acc / l).astype(o_ref.dtype)


def _attention(q_cat, k_cat, v):
    nq = SEQ // ATT_TQ
    return pl.pallas_call(
        _attn_kernel,
        grid=(BATCH, MLA_HEADS, nq),
        in_specs=[pl.BlockSpec((ATT_TQ, HEAD_PAD), lambda b, h, i: (b * nq + i, h)),
                  pl.BlockSpec((SEQ, HEAD_PAD), lambda b, h, i: (b, h)),
                  pl.BlockSpec((SEQ, V_DIM), lambda b, h, i: (b, h))],
        out_specs=pl.BlockSpec((ATT_TQ, V_DIM), lambda b, h, i: (b * nq + i, h)),
        out_shape=jax.ShapeDtypeStruct((TOKENS, MLA_HEADS * V_DIM), BF16),
        compiler_params=_params("parallel", "parallel", "arbitrary"),
        name="attention",
    )(q_cat, k_cat, v)


def _router_kernel(h_ref, w_ref, b_ref, o_ref):
    logits = jnp.dot(h_ref[...], w_ref[...], preferred_element_type=F32, precision=HIGHEST)
    logits = logits + b_ref[...]
    lane = lax.broadcasted_iota(jnp.int32, logits.shape, 1).astype(F32)
    logits = jnp.where(lane < N_EXPERTS, logits, -jnp.inf)
    m1 = jnp.max(logits, axis=-1, keepdims=True)
    i1 = jnp.min(jnp.where(logits == m1, lane, 128.0), axis=-1, keepdims=True)
    rest = jnp.where(lane == i1, -jnp.inf, logits)
    m2 = jnp.max(rest, axis=-1, keepdims=True)
    i2 = jnp.min(jnp.where(rest == m2, lane, 128.0), axis=-1, keepdims=True)
    e2 = jnp.exp(m2 - m1)
    w1 = 1.0 / (1.0 + e2)
    w2 = e2 / (1.0 + e2)
    out = jnp.where(lane == 0.0, i1,
                    jnp.where(lane == 1.0, i2,
                              jnp.where(lane == 2.0, w1, jnp.where(lane == 3.0, w2, 0.0))))
    o_ref[...] = out


def _router(h, w_pad, b_pad, tm=512):
    return pl.pallas_call(
        _router_kernel,
        grid=(TOKENS // tm,),
        in_specs=[pl.BlockSpec((tm, D_MODEL), lambda i: (i, 0)),
                  pl.BlockSpec((D_MODEL, 128), lambda i: (0, 0)),
                  pl.BlockSpec((1, 128), lambda i: (0, 0))],
        out_specs=pl.BlockSpec((tm, 128), lambda i: (i, 0)),
        out_shape=jax.ShapeDtypeStruct((TOKENS, 128), F32),
        compiler_params=_params("parallel"),
        name="router",
    )(h, w_pad, b_pad)


MOE_TM = 1024
MOE_TF = 256
MOE_ROWS = 2 * TOKENS + N_EXPERTS * MOE_TM
MOE_TILES = MOE_ROWS // MOE_TM


def _moe_kernel(te_ref, nt_ref, x_ref, wg_ref, wu_ref, wd_ref, o_ref, acc_ref):
    i = pl.program_id(0)
    f = pl.program_id(1)
    last = pl.num_programs(1) - 1
    valid = i < nt_ref[0]

    @pl.when(valid)
    def _():
        @pl.when(f == 0)
        def _():
            acc_ref[...] = jnp.zeros_like(acc_ref)

        a = x_ref[...]
        gate = jnp.dot(a, wg_ref[0], preferred_element_type=F32)
        up = jnp.dot(a, wu_ref[0], preferred_element_type=F32)
        mid = (_silu(gate) * up).astype(BF16)
        acc_ref[...] += jnp.dot(mid, wd_ref[0], preferred_element_type=F32)

        @pl.when(f == last)
        def _():
            o_ref[...] = acc_ref[...].astype(o_ref.dtype)

    @pl.when(jnp.logical_and(jnp.logical_not(valid), f == last))
    def _():
        o_ref[...] = jnp.zeros_like(o_ref)


def _moe_experts(tile_expert, n_tiles, x_sorted, wg, wu, wd):
    nf = D_FF_EXPERT // MOE_TF

    def row_map(i, f, te, nt):
        return (jnp.minimum(i, nt[0] - 1), 0)

    def f_of(i, f, nt):
        return jnp.where(i < nt[0], f, nf - 1)

    grid_spec = pltpu.PrefetchScalarGridSpec(
        num_scalar_prefetch=2,
        grid=(MOE_TILES, nf),
        in_specs=[
            pl.BlockSpec((MOE_TM, D_MODEL), row_map),
            pl.BlockSpec((1, D_MODEL, MOE_TF), lambda i, f, te, nt: (te[i], 0, f_of(i, f, nt))),
            pl.BlockSpec((1, D_MODEL, MOE_TF), lambda i, f, te, nt: (te[i], 0, f_of(i, f, nt))),
            pl.BlockSpec((1, MOE_TF, D_MODEL), lambda i, f, te, nt: (te[i], f_of(i, f, nt), 0)),
        ],
        out_specs=pl.BlockSpec((MOE_TM, D_MODEL), lambda i, f, te, nt: (i, 0)),
        scratch_shapes=[pltpu.VMEM((MOE_TM, D_MODEL), F32)],
    )
    return pl.pallas_call(
        _moe_kernel,
        grid_spec=grid_spec,
        out_shape=jax.ShapeDtypeStruct((MOE_ROWS, D_MODEL), BF16),
        compiler_params=_params("arbitrary", "arbitrary"),
        name="moe_experts",
    )(tile_expert, n_tiles, x_sorted, wg, wu, wd)


def _combine_ln_kernel(h_ref, y0_ref, y1_ref, w0_ref, w1_ref, g_ref, b_ref, o_ref):
    ff = w0_ref[...] * y0_ref[...].astype(F32) + w1_ref[...] * y1_ref[...].astype(F32)
    o_ref[...] = _layer_norm(DEEPNORM_ALPHA * h_ref[...] + ff, g_ref[...], b_ref[...])


def _combine_ln(h, y0, y1, w0, w1, g, b, tm=512):
    row = pl.BlockSpec((tm, D_MODEL), lambda i: (i, 0))
    col = pl.BlockSpec((tm, 1), lambda i: (i, 0))
    vec = pl.BlockSpec((1, D_MODEL), lambda i: (0, 0))
    return pl.pallas_call(
        _combine_ln_kernel,
        grid=(TOKENS // tm,),
        in_specs=[row, row, row, col, col, vec, vec],
        out_specs=row,
        out_shape=jax.ShapeDtypeStruct((TOKENS, D_MODEL), F32),
        compiler_params=_params("parallel"),
        name="combine_ln",
    )(h, y0, y1, w0, w1, g, b)


def _swap_halves(w):
    half = w.shape[-1] // 2
    return jnp.concatenate([w[..., half:], w[..., :half]], axis=-1)


def kernel(x, p, positions, ssm_w_in, ssm_conv_w, ssm_conv_b, ssm_dt_bias, ssm_a_log, ssm_d, ssm_norm_g, ssm_w_out, kv_w_down, kv_norm_g, kv_w_rope, kv_w_uk, kv_w_uv, mla_w_dq, mla_q_norm_g, mla_w_uq, mla_w_o, ffn_w_gate, ffn_w_up, ffn_w_down, moe_w_router, moe_b_router, moe_w_gate, moe_w_up, moe_w_down, ln1_g, ln1_b, ln2_g, ln2_b, ple_w_proj, ple_w_gate):
    h0 = x.reshape(TOKENS, D_MODEL)
    p = p.reshape(DEPTH, TOKENS, PLE_DIM)
    row = lambda v: v.reshape(1, -1)

    zx_cols = D_INNER + CONV_DIM
    w_zx = ssm_w_in[0][:, :zx_cols].astype(BF16)
    w_dt = ssm_w_in[0][:, zx_cols:]
    zx = _matmul(h0, w_zx, BF16, tm=1024, tn=1024)
    dt, acum = _dt_prep(h0, w_dt, row(ssm_dt_bias[0]), row(ssm_a_log[0]))
    xbc = _conv_silu(zx, ssm_conv_w[0], row(ssm_conv_b[0]))

    def by_group(v):
        return v.reshape(BATCH, SEQ, SSM_GROUPS, HEADS_PER_GROUP).transpose(0, 2, 1, 3)

    dt4 = by_group(dt)
    acum4 = by_group(acum)
    acumt4 = acum4.transpose(0, 1, 3, 2)
    d_full = row(jnp.repeat(ssm_d[0], SSM_HEAD_DIM))
    y = _ssd(xbc, zx, dt4, acum4, acumt4, d_full, row(ssm_norm_g[0]))
    h = _matmul_ln(y, ssm_w_out[0].astype(BF16), h0, row(ln1_g[0]), row(ln1_b[0]))

    h = _swiglu_ln(h, ffn_w_gate[0].astype(BF16), ffn_w_up[0].astype(BF16),
                   ffn_w_down[0].astype(BF16), row(ln2_g[0]), row(ln2_b[0]))
    h = _ple(h, p[0], ple_w_gate[0].astype(BF16), ple_w_proj[0].astype(BF16))

    inv_freq = ROPE_THETA ** (-jnp.arange(0, ROPE_DIM, 2, dtype=F32) / ROPE_DIM)
    table = _rope_table(positions.reshape(TOKENS, 1), row(jnp.tile(inv_freq, 4)))
    w_lat = jnp.concatenate([kv_w_down, mla_w_dq[0], kv_w_rope, _swap_halves(kv_w_rope)],
                            axis=1).astype(BF16)
    c_kv, c_q, k_rope = _latent(h, w_lat, row(kv_norm_g), row(mla_q_norm_g[0]), table)
    k_cat, v = _kv_up(c_kv, kv_w_uk.astype(BF16), kv_w_uv.astype(BF16), k_rope)
    w_uq = mla_w_uq[0].reshape(Q_RANK, MLA_HEADS, NOPE_DIM + ROPE_DIM)
    w_q = jnp.concatenate([w_uq, _swap_halves(w_uq[..., NOPE_DIM:])], axis=-1)
    w_q = w_q.reshape(Q_RANK, MLA_HEADS * HEAD_PAD).astype(BF16)
    q_cat = _q_up(c_q, w_q, table)
    o = _attention(q_cat, k_cat, v)
    h = _matmul_ln(o, mla_w_o[0].astype(BF16), h, row(ln1_g[1]), row(ln1_b[1]))

    w_r = jnp.pad(moe_w_router[0], ((0, 0), (0, 128 - N_EXPERTS)))
    b_r = jnp.pad(row(moe_b_router[0]), ((0, 0), (0, 128 - N_EXPERTS)))
    route = _router(h, w_r, b_r)
    expert = route[:, :2].astype(jnp.int32).reshape(-1)
    onehot = (expert[:, None] == jnp.arange(N_EXPERTS, dtype=jnp.int32)[None, :]).astype(jnp.int32)
    csum = jnp.cumsum(onehot, axis=0)
    rank = jnp.take_along_axis(csum, expert[:, None], axis=1)[:, 0] - 1
    counts = csum[-1]
    tiles_per = (counts + MOE_TM - 1) // MOE_TM
    tile_end = jnp.cumsum(tiles_per)
    start = (tile_end - tiles_per) * MOE_TM
    pos = start[expert] + rank
    n_tiles = tile_end[-1:].astype(jnp.int32)
    tile_ids = jnp.minimum(jnp.arange(MOE_TILES, dtype=jnp.int32), n_tiles[0] - 1)
    tile_expert = jnp.searchsorted(tile_end, tile_ids, side="right").astype(jnp.int32)
    token_of_slot = jnp.arange(2 * TOKENS, dtype=jnp.int32) // 2
    src = jnp.zeros((MOE_ROWS,), jnp.int32).at[pos].set(token_of_slot)
    x_sorted = jnp.take(h.astype(BF16), src, axis=0)
    y_sorted = _moe_experts(tile_expert, n_tiles, x_sorted, moe_w_gate[0].astype(BF16),
                            moe_w_up[0].astype(BF16), moe_w_down[0].astype(BF16))
    pos2 = pos.reshape(TOKENS, 2)
    y0 = jnp.take(y_sorted, pos2[:, 0], axis=0)
    y1 = jnp.take(y_sorted, pos2[:, 1], axis=0)
    h = _combine_ln(h, y0, y1, route[:, 2:3], route[:, 3:4], row(ln2_g[1]), row(ln2_b[1]))
    h = _ple(h, p[1], ple_w_gate[1].astype(BF16), ple_w_proj[1].astype(BF16))
    return h.reshape(BATCH, SEQ, D_MODEL)
```

```python
import functools

import jax
import jax.numpy as jnp
from jax import lax
from jax.experimental import pallas as pl
from jax.experimental.pallas import tpu as pltpu

F32 = jnp.float32
BF16 = jnp.bfloat16
HIGHEST = lax.Precision.HIGHEST

D_MODEL = 2048
BATCH = 16
SEQ = 2048
TOKENS = BATCH * SEQ
DEPTH = 2

D_INNER = 4096
SSM_HEAD_DIM = 64
SSM_HEADS = 64
SSM_GROUPS = 8
HEADS_PER_GROUP = 8
SSM_STATE = 128
GROUP_WIDTH = HEADS_PER_GROUP * SSM_HEAD_DIM
CONV_K = 4
CONV_DIM = D_INNER + 2 * SSM_GROUPS * SSM_STATE
SSM_CHUNK = 128
N_CHUNKS = SEQ // SSM_CHUNK

MLA_HEADS = 16
Q_RANK = 512
KV_RANK = 512
NOPE_DIM = 128
ROPE_DIM = 64
V_DIM = 128
ROPE_THETA = 10000.0
HEAD_PAD = 256

D_FF = 5632
N_EXPERTS = 8
D_FF_EXPERT = 2816

PLE_DIM = 256
DEEPNORM_ALPHA = (2.0 * DEPTH) ** 0.25
LN_EPS = 1e-5
RMS_EPS = 1e-6

VMEM_LIMIT = 56 * 1024 * 1024


def _params(*sem):
    return pltpu.CompilerParams(dimension_semantics=sem, vmem_limit_bytes=VMEM_LIMIT)


def _layer_norm(x, g, b):
    mu = jnp.mean(x, axis=-1, keepdims=True)
    xc = x - mu
    var = jnp.mean(xc * xc, axis=-1, keepdims=True)
    return xc * lax.rsqrt(var + LN_EPS) * g + b


def _rms_norm(x, g):
    return x * lax.rsqrt(jnp.mean(x * x, axis=-1, keepdims=True) + RMS_EPS) * g


def _silu(x):
    return x * (1.0 / (1.0 + jnp.exp(-x)))


def _sigmoid(x):
    return 1.0 / (1.0 + jnp.exp(-x))


def _rope_table_kernel(pos_ref, freq_ref, o_ref):
    ang = pos_ref[...].astype(F32) * freq_ref[...]
    lane = lax.broadcasted_iota(jnp.int32, ang.shape, 1)
    c = jnp.cos(ang)
    s = jnp.sin(ang)
    o_ref[...] = jnp.where(lane < 64, c, jnp.where(lane < 96, -s, s))


def _rope_table(pos_col, freq_row, tm=1024):
    return pl.pallas_call(
        _rope_table_kernel,
        grid=(TOKENS // tm,),
        in_specs=[pl.BlockSpec((tm, 1), lambda i: (i, 0)),
                  pl.BlockSpec((1, 128), lambda i: (0, 0))],
        out_specs=pl.BlockSpec((tm, 128), lambda i: (i, 0)),
        out_shape=jax.ShapeDtypeStruct((TOKENS, 128), F32),
        compiler_params=_params("parallel"),
        name="rope_table",
    )(pos_col, freq_row)


def _matmul_kernel(a_ref, w_ref, o_ref, a_bf):
    @pl.when(pl.program_id(1) == 0)
    def _():
        a_bf[...] = a_ref[...].astype(BF16)

    o_ref[...] = jnp.dot(a_bf[...], w_ref[...], preferred_element_type=F32).astype(o_ref.dtype)


def _matmul(a, w, out_dtype, tm, tn):
    m, k = a.shape
    n = w.shape[1]
    return pl.pallas_call(
        _matmul_kernel,
        grid=(m // tm, n // tn),
        in_specs=[pl.BlockSpec((tm, k), lambda i, j: (i, 0)),
                  pl.BlockSpec((k, tn), lambda i, j: (0, j))],
        out_specs=pl.BlockSpec((tm, tn), lambda i, j: (i, j)),
        out_shape=jax.ShapeDtypeStruct((m, n), out_dtype),
        scratch_shapes=[pltpu.VMEM((tm, k), BF16)],
        compiler_params=_params("parallel", "arbitrary"),
        name="matmul",
    )(a, w)


N_SPLIT = 3
N_WIDE = 4


def _split_bf16(v):
    parts = []
    for _ in range(N_SPLIT):
        part = v.astype(BF16)
        parts.append(part)
        v = v - part.astype(F32)
    return parts


def _dt_kernel(x_ref, w_ref, bias_ref, alog_ref, parts_ref, acum_ref, *, tm):
    raw = jnp.dot(x_ref[...], w_ref[...], preferred_element_type=F32, precision=HIGHEST)
    raw = raw + bias_ref[...]
    dt = jnp.maximum(raw, 0.0) + jnp.log1p(jnp.exp(-jnp.abs(raw)))
    a = -jnp.exp(alog_ref[...]) * dt
    row = lax.broadcasted_iota(jnp.int32, (SSM_CHUNK, SSM_CHUNK), 0)
    col = lax.broadcasted_iota(jnp.int32, (SSM_CHUNK, SSM_CHUNK), 1)
    tril = jnp.where(row >= col, 1.0, 0.0).astype(F32)
    for c in range(tm // SSM_CHUNK):
        sl = slice(c * SSM_CHUNK, (c + 1) * SSM_CHUNK)
        ac = jnp.dot(tril, a[sl, :], preferred_element_type=F32, precision=HIGHEST)
        acum_ref[sl, :] = ac
        ac_end = ac[SSM_CHUNK - 1:SSM_CHUNK, :]
        for qty, val in enumerate((dt[sl, :], jnp.exp(ac), jnp.exp(ac_end - ac), ac)):
            for part, piece in enumerate(_split_bf16(val)):
                parts_ref[qty * N_SPLIT + part, sl, :] = piece


def _dt_prep(x, w_dt, dt_bias, a_log, tm=512):
    n_parts = N_WIDE * N_SPLIT
    return pl.pallas_call(
        functools.partial(_dt_kernel, tm=tm),
        grid=(TOKENS // tm,),
        in_specs=[pl.BlockSpec((tm, D_MODEL), lambda i: (i, 0)),
                  pl.BlockSpec((D_MODEL, SSM_HEADS), lambda i: (0, 0)),
                  pl.BlockSpec((1, SSM_HEADS), lambda i: (0, 0)),
                  pl.BlockSpec((1, SSM_HEADS), lambda i: (0, 0))],
        out_specs=[pl.BlockSpec((n_parts, tm, SSM_HEADS), lambda i: (0, i, 0)),
                   pl.BlockSpec((tm, SSM_HEADS), lambda i: (i, 0))],
        out_shape=[jax.ShapeDtypeStruct((n_parts, TOKENS, SSM_HEADS), BF16),
                   jax.ShapeDtypeStruct((TOKENS, SSM_HEADS), F32)],
        compiler_params=_params("parallel"),
        name="dt_prep",
    )(x, w_dt, dt_bias, a_log)


CONV_ROWS = 256
CONV_HALO = 16


def _conv_kernel(u_ref, w_ref, b_ref, o_ref):
    w = w_ref[...]
    b = b_ref[...]
    tc = u_ref.shape[1]
    for i in range(SEQ // CONV_ROWS):
        r = i * CONV_ROWS
        cur = u_ref[r:r + CONV_ROWS, :].astype(F32)
        if i == 0:
            halo = jnp.zeros((CONV_HALO, tc), F32)
        else:
            halo = u_ref[r - CONV_HALO:r, :].astype(F32)
        win = jnp.concatenate([halo, cur], axis=0)
        acc = b + w[CONV_K - 1:CONV_K, :] * cur
        for k in range(CONV_K - 1):
            shifted = pltpu.roll(win, CONV_K - 1 - k, axis=0)[CONV_HALO:, :]
            acc = acc + w[k:k + 1, :] * shifted
        o_ref[r:r + CONV_ROWS, :] = _silu(acc).astype(o_ref.dtype)


def _conv_silu(zx, conv_w, conv_b, tc=512):
    col0 = D_INNER // tc
    return pl.pallas_call(
        _conv_kernel,
        grid=(BATCH, CONV_DIM // tc),
        in_specs=[pl.BlockSpec((SEQ, tc), lambda b, j: (b, col0 + j)),
                  pl.BlockSpec((CONV_K, tc), lambda b, j: (0, j)),
                  pl.BlockSpec((1, tc), lambda b, j: (0, j))],
        out_specs=pl.BlockSpec((SEQ, tc), lambda b, j: (b, j)),
        out_shape=jax.ShapeDtypeStruct((TOKENS, CONV_DIM), BF16),
        compiler_params=_params("parallel", "parallel"),
        name="conv_silu",
    )(zx, conv_w, conv_b)


WIDE_ROWS = N_WIDE * N_SPLIT * HEADS_PER_GROUP
WIDE_ACUM = 3 * GROUP_WIDTH
WIDE_COLS = WIDE_ACUM + HEADS_PER_GROUP * SSM_CHUNK


def _widen_matrix():
    r = jnp.arange(WIDE_ROWS)[:, None]
    c = jnp.arange(WIDE_COLS)[None, :]
    qty = r // (N_SPLIT * HEADS_PER_GROUP)
    head = r % HEADS_PER_GROUP
    chan = (c // GROUP_WIDTH == qty) & ((c % GROUP_WIDTH) // SSM_HEAD_DIM == head) & (c < WIDE_ACUM)
    copy = (qty == N_WIDE - 1) & (c >= WIDE_ACUM) & ((c - WIDE_ACUM) // SSM_CHUNK == head)
    return (chan | copy).astype(BF16)


def _ssd_kernel(x_ref, b_ref, c_ref, z_ref, parts_ref, acumt_ref, widen_ref, d_ref, g_ref,
                o_ref, state_ref):
    q = SSM_CHUNK
    state_ref[...] = jnp.zeros_like(state_ref)
    trow = lax.broadcasted_iota(jnp.int32, (q, q), 0)
    tcol = lax.broadcasted_iota(jnp.int32, (q, q), 1)
    causal = trow >= tcol
    first_head = lax.broadcasted_iota(jnp.int32, (q, 2 * SSM_HEAD_DIM), 1) < SSM_HEAD_DIM

    def chunk(c, carry):
        rows = pl.ds(pl.multiple_of(c * q, q), q)
        xf = x_ref[rows, :].astype(F32)
        bm = b_ref[rows, :]
        cm = c_ref[rows, :]
        act = acumt_ref[0, 0, c]
        wide = jnp.dot(parts_ref[0, 0, rows, :], widen_ref[...], preferred_element_type=F32)
        xdt = xf * wide[:, :GROUP_WIDTH]
        decay_out = wide[:, GROUP_WIDTH:2 * GROUP_WIDTH]
        to_end = wide[:, 2 * GROUP_WIDTH:WIDE_ACUM]
        xdt_bf = xdt.astype(BF16)

        cb = lax.dot_general(cm, bm, (((1,), (1,)), ((), ())), preferred_element_type=F32)
        state = state_ref[...]
        y = jnp.dot(cm, state.astype(BF16), preferred_element_type=F32) * decay_out

        pairs = []
        for pr in range(HEADS_PER_GROUP // 2):
            xpair = xdt_bf[:, pr * 128:(pr + 1) * 128]
            ms = []
            for j in (2 * pr, 2 * pr + 1):
                col0 = WIDE_ACUM + j * q
                seg = jnp.where(causal, wide[:, col0:col0 + q] - act[j:j + 1, :], -jnp.inf)
                ms.append((cb * jnp.exp(seg)).astype(BF16))
            zero = jnp.zeros_like(xpair)
            rhs = jnp.concatenate([jnp.where(first_head, xpair, zero),
                                   jnp.where(first_head, zero, xpair)], axis=0)
            pairs.append(jnp.dot(jnp.concatenate(ms, axis=1), rhs, preferred_element_type=F32))
        y = y + jnp.concatenate(pairs, axis=1)

        upd = lax.dot_general(bm, (xdt * to_end).astype(BF16), (((0,), (0,)), ((), ())),
                              preferred_element_type=F32)
        state_ref[...] = state * decay_out[q - 1:q, :] + upd

        y = y + d_ref[...] * xf
        y = y * _silu(z_ref[rows, :].astype(F32))
        o_ref[rows, :] = _rms_norm(y, g_ref[...]).astype(o_ref.dtype)
        return carry

    lax.fori_loop(0, N_CHUNKS, chunk, 0)


def _ssd(xbc, zx, parts, acumt, d_full, norm_g):
    x_cols = D_INNER // SSM_STATE
    nb = SSM_GROUPS
    return pl.pallas_call(
        _ssd_kernel,
        grid=(BATCH, SSM_GROUPS),
        in_specs=[
            pl.BlockSpec((SEQ, GROUP_WIDTH), lambda b, g: (b, g)),
            pl.BlockSpec((SEQ, SSM_STATE), lambda b, g: (b, x_cols + g)),
            pl.BlockSpec((SEQ, SSM_STATE), lambda b, g: (b, x_cols + nb + g)),
            pl.BlockSpec((SEQ, GROUP_WIDTH), lambda b, g: (b, g)),
            pl.BlockSpec((1, 1, SEQ, WIDE_ROWS), lambda b, g: (b, g, 0, 0)),
            pl.BlockSpec((1, 1, N_CHUNKS, HEADS_PER_GROUP, SSM_CHUNK), lambda b, g: (b, g, 0, 0, 0)),
            pl.BlockSpec((WIDE_ROWS, WIDE_COLS), lambda b, g: (0, 0)),
            pl.BlockSpec((1, GROUP_WIDTH), lambda b, g: (0, g)),
            pl.BlockSpec((1, GROUP_WIDTH), lambda b, g: (0, g)),
        ],
        out_specs=pl.BlockSpec((SEQ, GROUP_WIDTH), lambda b, g: (b, g)),
        out_shape=jax.ShapeDtypeStruct((TOKENS, D_INNER), BF16),
        scratch_shapes=[pltpu.VMEM((SSM_STATE, GROUP_WIDTH), F32)],
        compiler_params=_params("parallel", "parallel"),
        name="ssd_scan",
    )(xbc, xbc, xbc, zx, parts, acumt, _widen_matrix(), d_full, norm_g)


def _matmul_ln_kernel(a_ref, w_ref, res_ref, g_ref, b_ref, o_ref, acc_ref):
    k = pl.program_id(1)

    @pl.when(k == 0)
    def _():
        acc_ref[...] = jnp.zeros_like(acc_ref)

    acc_ref[...] += jnp.dot(a_ref[...], w_ref[...], preferred_element_type=F32)

    @pl.when(k == pl.num_programs(1) - 1)
    def _():
        x = DEEPNORM_ALPHA * res_ref[...] + acc_ref[...]
        o_ref[...] = _layer_norm(x, g_ref[...], b_ref[...])


def _matmul_ln(a, w, res, g, b, tm=512, tk=1024):
    m, k = a.shape
    n = w.shape[1]
    return pl.pallas_call(
        _matmul_ln_kernel,
        grid=(m // tm, k // tk),
        in_specs=[pl.BlockSpec((tm, tk), lambda i, kk: (i, kk)),
                  pl.BlockSpec((tk, n), lambda i, kk: (kk, 0)),
                  pl.BlockSpec((tm, n), lambda i, kk: (i, 0)),
                  pl.BlockSpec((1, n), lambda i, kk: (0, 0)),
                  pl.BlockSpec((1, n), lambda i, kk: (0, 0))],
        out_specs=pl.BlockSpec((tm, n), lambda i, kk: (i, 0)),
        out_shape=jax.ShapeDtypeStruct((m, n), F32),
        scratch_shapes=[pltpu.VMEM((tm, n), F32)],
        compiler_params=_params("parallel", "arbitrary"),
        name="matmul_ln",
    )(a, w, res, g, b)


def _swiglu_ln_kernel(h_ref, wg_ref, wu_ref, wd_ref, g_ref, b_ref, o_ref, h_bf, acc_ref):
    f = pl.program_id(1)

    @pl.when(f == 0)
    def _():
        h_bf[...] = h_ref[...].astype(BF16)
        acc_ref[...] = jnp.zeros_like(acc_ref)

    a = h_bf[...]
    gate = jnp.dot(a, wg_ref[...], preferred_element_type=F32)
    up = jnp.dot(a, wu_ref[...], preferred_element_type=F32)
    mid = (_silu(gate) * up).astype(BF16)
    acc_ref[...] += jnp.dot(mid, wd_ref[...], preferred_element_type=F32)

    @pl.when(f == pl.num_programs(1) - 1)
    def _():
        x = DEEPNORM_ALPHA * h_ref[...] + acc_ref[...]
        o_ref[...] = _layer_norm(x, g_ref[...], b_ref[...])


def _swiglu_ln(h, wg, wu, wd, g, b, tm=512, tf=512):
    m = h.shape[0]
    ff = wg.shape[1]
    return pl.pallas_call(
        _swiglu_ln_kernel,
        grid=(m // tm, ff // tf),
        in_specs=[pl.BlockSpec((tm, D_MODEL), lambda i, f: (i, 0)),
                  pl.BlockSpec((D_MODEL, tf), lambda i, f: (0, f)),
                  pl.BlockSpec((D_MODEL, tf), lambda i, f: (0, f)),
                  pl.BlockSpec((tf, D_MODEL), lambda i, f: (f, 0)),
                  pl.BlockSpec((1, D_MODEL), lambda i, f: (0, 0)),
                  pl.BlockSpec((1, D_MODEL), lambda i, f: (0, 0))],
        out_specs=pl.BlockSpec((tm, D_MODEL), lambda i, f: (i, 0)),
        out_shape=jax.ShapeDtypeStruct((m, D_MODEL), F32),
        scratch_shapes=[pltpu.VMEM((tm, D_MODEL), BF16), pltpu.VMEM((tm, D_MODEL), F32)],
        compiler_params=_params("parallel", "arbitrary"),
        name="swiglu_ln",
    )(h, wg, wu, wd, g, b)


def _ple_kernel(h_ref, p_ref, wg_ref, wp_ref, o_ref, h_bf, *, tn):
    j = pl.program_id(1)

    @pl.when(j == 0)
    def _():
        h_bf[...] = h_ref[...].astype(BF16)

    gate = jnp.dot(h_bf[...], wg_ref[...], preferred_element_type=F32)
    proj = jnp.dot(p_ref[...].astype(BF16), wp_ref[...], preferred_element_type=F32)
    res = h_ref[:, pl.ds(pl.multiple_of(j * tn, tn), tn)]
    o_ref[...] = res + _sigmoid(gate) * proj


def _ple(h, p, wg, wp, tm=512, tn=1024):
    m = h.shape[0]
    return pl.pallas_call(
        functools.partial(_ple_kernel, tn=tn),
        grid=(m // tm, D_MODEL // tn),
        in_specs=[pl.BlockSpec((tm, D_MODEL), lambda i, j: (i, 0)),
                  pl.BlockSpec((tm, PLE_DIM), lambda i, j: (i, 0)),
                  pl.BlockSpec((D_MODEL, tn), lambda i, j: (0, j)),
                  pl.BlockSpec((PLE_DIM, tn), lambda i, j: (0, j))],
        out_specs=pl.BlockSpec((tm, tn), lambda i, j: (i, j)),
        out_shape=jax.ShapeDtypeStruct((m, D_MODEL), F32),
        scratch_shapes=[pltpu.VMEM((tm, D_MODEL), BF16)],
        compiler_params=_params("parallel", "arbitrary"),
        name="ple",
    )(h, p, wg, wp)


def _latent_kernel(h_ref, w_ref, gkv_ref, gq_ref, tab_ref, ckv_ref, cq_ref, kr_ref):
    r = jnp.dot(h_ref[...].astype(BF16), w_ref[...], preferred_element_type=F32)
    ckv_ref[...] = _rms_norm(r[:, :KV_RANK], gkv_ref[...]).astype(BF16)
    cq_ref[...] = _rms_norm(r[:, KV_RANK:KV_RANK + Q_RANK], gq_ref[...]).astype(BF16)
    t = r[:, KV_RANK + Q_RANK:] * tab_ref[...]
    rot = t + pltpu.roll(t, ROPE_DIM, axis=1)
    lane = lax.broadcasted_iota(jnp.int32, rot.shape, 1)
    kr_ref[...] = jnp.where(lane < ROPE_DIM, rot, 0.0).astype(BF16)


def _latent(h, w_cat, g_kv, g_q, table, tm=512):
    n = w_cat.shape[1]
    return pl.pallas_call(
        _latent_kernel,
        grid=(TOKENS // tm,),
        in_specs=[pl.BlockSpec((tm, D_MODEL), lambda i: (i, 0)),
                  pl.BlockSpec((D_MODEL, n), lambda i: (0, 0)),
                  pl.BlockSpec((1, KV_RANK), lambda i: (0, 0)),
                  pl.BlockSpec((1, Q_RANK), lambda i: (0, 0)),
                  pl.BlockSpec((tm, 128), lambda i: (i, 0))],
        out_specs=[pl.BlockSpec((tm, KV_RANK), lambda i: (i, 0)),
                   pl.BlockSpec((tm, Q_RANK), lambda i: (i, 0)),
                   pl.BlockSpec((tm, 128), lambda i: (i, 0))],
        out_shape=[jax.ShapeDtypeStruct((TOKENS, KV_RANK), BF16),
                   jax.ShapeDtypeStruct((TOKENS, Q_RANK), BF16),
                   jax.ShapeDtypeStruct((TOKENS, 128), BF16)],
        compiler_params=_params("parallel"),
        name="latent",
    )(h, w_cat, g_kv, g_q, table)


def _kv_up_kernel(c_ref, wk_ref, wv_ref, kr_ref, k_ref, v_ref):
    c = c_ref[...]
    kn = jnp.dot(c, wk_ref[...], preferred_element_type=F32).astype(BF16)
    v_ref[...] = jnp.dot(c, wv_ref[...], preferred_element_type=F32).astype(BF16)
    kr = kr_ref[...]
    for h in range(MLA_HEADS):
        k_ref[:, h * HEAD_PAD:h * HEAD_PAD + NOPE_DIM] = kn[:, h * NOPE_DIM:(h + 1) * NOPE_DIM]
        k_ref[:, h * HEAD_PAD + NOPE_DIM:(h + 1) * HEAD_PAD] = kr


def _kv_up(c_kv, w_uk, w_uv, k_rope, tm=512):
    return pl.pallas_call(
        _kv_up_kernel,
        grid=(TOKENS // tm,),
        in_specs=[pl.BlockSpec((tm, KV_RANK), lambda i: (i, 0)),
                  pl.BlockSpec((KV_RANK, MLA_HEADS * NOPE_DIM), lambda i: (0, 0)),
                  pl.BlockSpec((KV_RANK, MLA_HEADS * V_DIM), lambda i: (0, 0)),
                  pl.BlockSpec((tm, 128), lambda i: (i, 0))],
        out_specs=[pl.BlockSpec((tm, MLA_HEADS * HEAD_PAD), lambda i: (i, 0)),
                   pl.BlockSpec((tm, MLA_HEADS * V_DIM), lambda i: (i, 0))],
        out_shape=[jax.ShapeDtypeStruct((TOKENS, MLA_HEADS * HEAD_PAD), BF16),
                   jax.ShapeDtypeStruct((TOKENS, MLA_HEADS * V_DIM), BF16)],
        compiler_params=_params("parallel"),
        name="kv_up",
    )(c_kv, w_uk, w_uv, k_rope)


def _q_up_kernel(c_ref, w_ref, tab_ref, q_ref):
    scale = (NOPE_DIM + ROPE_DIM) ** -0.5
    r = jnp.dot(c_ref[...], w_ref[...], preferred_element_type=F32)
    tab = tab_ref[...]
    for h in range(MLA_HEADS):
        base = h * HEAD_PAD
        q_ref[:, base:base + NOPE_DIM] = (r[:, base:base + NOPE_DIM] * scale).astype(BF16)
        t = r[:, base + NOPE_DIM:base + HEAD_PAD] * tab
        rot = (t + pltpu.roll(t, ROPE_DIM, axis=1)) * scale
        q_ref[:, base + NOPE_DIM:base + HEAD_PAD] = rot.astype(BF16)


def _q_up(c_q, w_q, table, tm=512):
    return pl.pallas_call(
        _q_up_kernel,
        grid=(TOKENS // tm,),
        in_specs=[pl.BlockSpec((tm, Q_RANK), lambda i: (i, 0)),
                  pl.BlockSpec((Q_RANK, MLA_HEADS * HEAD_PAD), lambda i: (0, 0)),
                  pl.BlockSpec((tm, 128), lambda i: (i, 0))],
        out_specs=pl.BlockSpec((tm, MLA_HEADS * HEAD_PAD), lambda i: (i, 0)),
        out_shape=jax.ShapeDtypeStruct((TOKENS, MLA_HEADS * HEAD_PAD), BF16),
        compiler_params=_params("parallel"),
        name="q_up",
    )(c_q, w_q, table)


ATT_TQ = 512
ATT_TK = 512


def _attn_kernel(q_ref, k_ref, v_ref, o_ref):
    row = lax.broadcasted_iota(jnp.int32, (ATT_TQ, ATT_TK), 0)
    col = lax.broadcasted_iota(jnp.int32, (ATT_TQ, ATT_TK), 1)
    causal = row >= col
    for qi in range(SEQ // ATT_TQ):
        q = q_ref[qi * ATT_TQ:(qi + 1) * ATT_TQ, :]
        m = l = acc = None
        for kj in range(qi + 1):
            k = k_ref[kj * ATT_TK:(kj + 1) * ATT_TK, :]
            v = v_ref[kj * ATT_TK:(kj + 1) * ATT_TK, :]
            s = lax.dot_general(q, k, (((1,), (1,)), ((), ())), preferred_element_type=F32)
            if kj == qi:
                s = jnp.where(causal, s, -jnp.inf)
            s_max = jnp.max(s, axis=-1, keepdims=True)
            if kj == 0:
                m = s_max
                p = jnp.exp(s - m)
                l = jnp.sum(p, axis=-1, keepdims=True)
                acc = jnp.dot(p.astype(BF16), v, preferred_element_type=F32)
            else:
                m_new = jnp.maximum(m, s_max)
                alpha = jnp.exp(m - m_new)
                p = jnp.exp(s - m_new)
                l = alpha * l + jnp.sum(p, axis=-1, keepdims=True)
                acc = alpha * acc + jnp.dot(p.astype(BF16), v, preferred_element_type=F32)
                m = m_new
        o_ref[qi * ATT_TQ:(qi + 1) * ATT_TQ, :] = (acc / l).astype(o_ref.dtype)


def _attention(q_cat, k_cat, v):
    return pl.pallas_call(
        _attn_kernel,
        grid=(BATCH, MLA_HEADS),
        in_specs=[pl.BlockSpec((SEQ, HEAD_PAD), lambda b, h: (b, h)),
                  pl.BlockSpec((SEQ, HEAD_PAD), lambda b, h: (b, h)),
                  pl.BlockSpec((SEQ, V_DIM), lambda b, h: (b, h))],
        out_specs=pl.BlockSpec((SEQ, V_DIM), lambda b, h: (b, h)),
        out_shape=jax.ShapeDtypeStruct((TOKENS, MLA_HEADS * V_DIM), BF16),
        compiler_params=_params("parallel", "parallel"),
        name="attention",
    )(q_cat, k_cat, v)


def _router_kernel(h_ref, w_ref, b_ref, o_ref):
    logits = jnp.dot(h_ref[...], w_ref[...], preferred_element_type=F32, precision=HIGHEST)
    logits = logits + b_ref[...]
    lane = lax.broadcasted_iota(jnp.int32, logits.shape, 1).astype(F32)
    logits = jnp.where(lane < N_EXPERTS, logits, -jnp.inf)
    m1 = jnp.max(logits, axis=-1, keepdims=True)
    i1 = jnp.min(jnp.where(logits == m1, lane, 128.0), axis=-1, keepdims=True)
    rest = jnp.where(lane == i1, -jnp.inf, logits)
    m2 = jnp.max(rest, axis=-1, keepdims=True)
    i2 = jnp.min(jnp.where(rest == m2, lane, 128.0), axis=-1, keepdims=True)
    e2 = jnp.exp(m2 - m1)
    w1 = 1.0 / (1.0 + e2)
    w2 = e2 / (1.0 + e2)
    out = jnp.where(lane == 0.0, i1,
                    jnp.where(lane == 1.0, i2,
                              jnp.where(lane == 2.0, w1, jnp.where(lane == 3.0, w2, 0.0))))
    o_ref[...] = out


def _router(h, w_pad, b_pad, tm=512):
    return pl.pallas_call(
        _router_kernel,
        grid=(TOKENS // tm,),
        in_specs=[pl.BlockSpec((tm, D_MODEL), lambda i: (i, 0)),
                  pl.BlockSpec((D_MODEL, 128), lambda i: (0, 0)),
                  pl.BlockSpec((1, 128), lambda i: (0, 0))],
        out_specs=pl.BlockSpec((tm, 128), lambda i: (i, 0)),
        out_shape=jax.ShapeDtypeStruct((TOKENS, 128), F32),
        compiler_params=_params("parallel"),
        name="router",
    )(h, w_pad, b_pad)


MOE_TM = 1024
MOE_TF = 256
MOE_ROWS = 2 * TOKENS + N_EXPERTS * MOE_TM
MOE_TILES = MOE_ROWS // MOE_TM


def _moe_kernel(te_ref, nt_ref, x_ref, wg_ref, wu_ref, wd_ref, o_ref, acc_ref):
    i = pl.program_id(0)
    f = pl.program_id(1)
    last = pl.num_programs(1) - 1
    valid = i < nt_ref[0]

    @pl.when(valid)
    def _():
        @pl.when(f == 0)
        def _():
            acc_ref[...] = jnp.zeros_like(acc_ref)

        a = x_ref[...]
        gate = jnp.dot(a, wg_ref[0], preferred_element_type=F32)
        up = jnp.dot(a, wu_ref[0], preferred_element_type=F32)
        mid = (_silu(gate) * up).astype(BF16)
        acc_ref[...] += jnp.dot(mid, wd_ref[0], preferred_element_type=F32)

        @pl.when(f == last)
        def _():
            o_ref[...] = acc_ref[...].astype(o_ref.dtype)

    @pl.when(jnp.logical_and(jnp.logical_not(valid), f == last))
    def _():
        o_ref[...] = jnp.zeros_like(o_ref)


def _moe_experts(tile_expert, n_tiles, x_sorted, wg, wu, wd):
    nf = D_FF_EXPERT // MOE_TF

    def row_map(i, f, te, nt):
        return (jnp.minimum(i, nt[0] - 1), 0)

    def f_of(i, f, nt):
        return jnp.where(i < nt[0], f, nf - 1)

    grid_spec = pltpu.PrefetchScalarGridSpec(
        num_scalar_prefetch=2,
        grid=(MOE_TILES, nf),
        in_specs=[
            pl.BlockSpec((MOE_TM, D_MODEL), row_map),
            pl.BlockSpec((1, D_MODEL, MOE_TF), lambda i, f, te, nt: (te[i], 0, f_of(i, f, nt))),
            pl.BlockSpec((1, D_MODEL, MOE_TF), lambda i, f, te, nt: (te[i], 0, f_of(i, f, nt))),
            pl.BlockSpec((1, MOE_TF, D_MODEL), lambda i, f, te, nt: (te[i], f_of(i, f, nt), 0)),
        ],
        out_specs=pl.BlockSpec((MOE_TM, D_MODEL), lambda i, f, te, nt: (i, 0)),
        scratch_shapes=[pltpu.VMEM((MOE_TM, D_MODEL), F32)],
    )
    return pl.pallas_call(
        _moe_kernel,
        grid_spec=grid_spec,
        out_shape=jax.ShapeDtypeStruct((MOE_ROWS, D_MODEL), BF16),
        compiler_params=_params("arbitrary", "arbitrary"),
        name="moe_experts",
    )(tile_expert, n_tiles, x_sorted, wg, wu, wd)


def _combine_ln_kernel(h_ref, y0_ref, y1_ref, w0_ref, w1_ref, g_ref, b_ref, o_ref):
    ff = w0_ref[...] * y0_ref[...].astype(F32) + w1_ref[...] * y1_ref[...].astype(F32)
    o_ref[...] = _layer_norm(DEEPNORM_ALPHA * h_ref[...] + ff, g_ref[...], b_ref[...])


def _combine_ln(h, y0, y1, w0, w1, g, b, tm=512):
    row = pl.BlockSpec((tm, D_MODEL), lambda i: (i, 0))
    col = pl.BlockSpec((tm, 1), lambda i: (i, 0))
    vec = pl.BlockSpec((1, D_MODEL), lambda i: (0, 0))
    return pl.pallas_call(
        _combine_ln_kernel,
        grid=(TOKENS // tm,),
        in_specs=[row, row, row, col, col, vec, vec],
        out_specs=row,
        out_shape=jax.ShapeDtypeStruct((TOKENS, D_MODEL), F32),
        compiler_params=_params("parallel"),
        name="combine_ln",
    )(h, y0, y1, w0, w1, g, b)


def _swap_halves(w):
    half = w.shape[-1] // 2
    return jnp.concatenate([w[..., half:], w[..., :half]], axis=-1)


def kernel(x, p, positions, ssm_w_in, ssm_conv_w, ssm_conv_b, ssm_dt_bias, ssm_a_log, ssm_d, ssm_norm_g, ssm_w_out, kv_w_down, kv_norm_g, kv_w_rope, kv_w_uk, kv_w_uv, mla_w_dq, mla_q_norm_g, mla_w_uq, mla_w_o, ffn_w_gate, ffn_w_up, ffn_w_down, moe_w_router, moe_b_router, moe_w_gate, moe_w_up, moe_w_down, ln1_g, ln1_b, ln2_g, ln2_b, ple_w_proj, ple_w_gate):
    h0 = x.reshape(TOKENS, D_MODEL)
    p = p.reshape(DEPTH, TOKENS, PLE_DIM)
    row = lambda v: v.reshape(1, -1)

    zx_cols = D_INNER + CONV_DIM
    w_zx = ssm_w_in[0][:, :zx_cols].astype(BF16)
    w_dt = ssm_w_in[0][:, zx_cols:]
    zx = _matmul(h0, w_zx, BF16, tm=1024, tn=1024)
    parts, acum = _dt_prep(h0, w_dt, row(ssm_dt_bias[0]), row(ssm_a_log[0]))
    xbc = _conv_silu(zx, ssm_conv_w[0], row(ssm_conv_b[0]))
    parts = parts.reshape(N_WIDE * N_SPLIT, BATCH, SEQ, SSM_GROUPS, HEADS_PER_GROUP)
    parts = parts.transpose(1, 3, 2, 0, 4).reshape(BATCH, SSM_GROUPS, SEQ, WIDE_ROWS)
    acumt = acum.reshape(BATCH, N_CHUNKS, SSM_CHUNK, SSM_GROUPS, HEADS_PER_GROUP)
    acumt = acumt.transpose(0, 3, 1, 4, 2)
    d_full = row(jnp.repeat(ssm_d[0], SSM_HEAD_DIM))
    y = _ssd(xbc, zx, parts, acumt, d_full, row(ssm_norm_g[0]))
    h = _matmul_ln(y, ssm_w_out[0].astype(BF16), h0, row(ln1_g[0]), row(ln1_b[0]))

    h = _swiglu_ln(h, ffn_w_gate[0].astype(BF16), ffn_w_up[0].astype(BF16),
                   ffn_w_down[0].astype(BF16), row(ln2_g[0]), row(ln2_b[0]))
    h = _ple(h, p[0], ple_w_gate[0].astype(BF16), ple_w_proj[0].astype(BF16))

    inv_freq = ROPE_THETA ** (-jnp.arange(0, ROPE_DIM, 2, dtype=F32) / ROPE_DIM)
    table = _rope_table(positions.reshape(TOKENS, 1), row(jnp.tile(inv_freq, 4)))
    w_lat = jnp.concatenate([kv_w_down, mla_w_dq[0], kv_w_rope, _swap_halves(kv_w_rope)],
                            axis=1).astype(BF16)
    c_kv, c_q, k_rope = _latent(h, w_lat, row(kv_norm_g), row(mla_q_norm_g[0]), table)
    k_cat, v = _kv_up(c_kv, kv_w_uk.astype(BF16), kv_w_uv.astype(BF16), k_rope)
    w_uq = mla_w_uq[0].reshape(Q_RANK, MLA_HEADS, NOPE_DIM + ROPE_DIM)
    w_q = jnp.concatenate([w_uq, _swap_halves(w_uq[..., NOPE_DIM:])], axis=-1)
    w_q = w_q.reshape(Q_RANK, MLA_HEADS * HEAD_PAD).astype(BF16)
    q_cat = _q_up(c_q, w_q, table)
    o = _attention(q_cat, k_cat, v)
    h = _matmul_ln(o, mla_w_o[0].astype(BF16), h, row(ln1_g[1]), row(ln1_b[1]))

    w_r = jnp.pad(moe_w_router[0], ((0, 0), (0, 128 - N_EXPERTS)))
    b_r = jnp.pad(row(moe_b_router[0]), ((0, 0), (0, 128 - N_EXPERTS)))
    route = _router(h, w_r, b_r)
    expert = route[:, :2].astype(jnp.int32).reshape(-1)
    onehot = (expert[:, None] == jnp.arange(N_EXPERTS, dtype=jnp.int32)[None, :]).astype(jnp.int32)
    csum = jnp.cumsum(onehot, axis=0)
    rank = jnp.take_along_axis(csum, expert[:, None], axis=1)[:, 0] - 1
    counts = csum[-1]
    tiles_per = (counts + MOE_TM - 1) // MOE_TM
    tile_end = jnp.cumsum(tiles_per)
    start = (tile_end - tiles_per) * MOE_TM
    pos = start[expert] + rank
    n_tiles = tile_end[-1:].astype(jnp.int32)
    tile_ids = jnp.minimum(jnp.arange(MOE_TILES, dtype=jnp.int32), n_tiles[0] - 1)
    tile_expert = jnp.searchsorted(tile_end, tile_ids, side="right").astype(jnp.int32)
    token_of_slot = jnp.arange(2 * TOKENS, dtype=jnp.int32) // 2
    src = jnp.zeros((MOE_ROWS,), jnp.int32).at[pos].set(token_of_slot)
    x_sorted = jnp.take(h.astype(BF16), src, axis=0)
    y_sorted = _moe_experts(tile_expert, n_tiles, x_sorted, moe_w_gate[0].astype(BF16),
                            moe_w_up[0].astype(BF16), moe_w_down[0].astype(BF16))
    pos2 = pos.reshape(TOKENS, 2)
    y0 = jnp.take(y_sorted, pos2[:, 0], axis=0)
    y1 = jnp.take(y_sorted, pos2[:, 1], axis=0)
    h = _combine_ln(h, y0, y1, route[:, 2:3], route[:, 3:4], row(ln2_g[1]), row(ln2_b[1]))
    h = _ple(h, p[1], ple_w_gate[1].astype(BF16), ple_w_proj[1].astype(BF16))
    return h.reshape(BATCH, SEQ, D_MODEL)
```

```python
import functools

import jax
import jax.numpy as jnp
from jax import lax
from jax.experimental import pallas as pl
from jax.experimental.pallas import tpu as pltpu

F32 = jnp.float32
BF16 = jnp.bfloat16
HIGHEST = lax.Precision.HIGHEST

D_MODEL = 2048
BATCH = 16
SEQ = 2048
TOKENS = BATCH * SEQ
DEPTH = 2

D_INNER = 4096
SSM_HEAD_DIM = 64
SSM_HEADS = 64
SSM_GROUPS = 8
HEADS_PER_GROUP = 8
SSM_STATE = 128
GROUP_WIDTH = HEADS_PER_GROUP * SSM_HEAD_DIM
CONV_K = 4
CONV_DIM = D_INNER + 2 * SSM_GROUPS * SSM_STATE
SSM_CHUNK = 128
N_CHUNKS = SEQ // SSM_CHUNK

MLA_HEADS = 16
Q_RANK = 512
KV_RANK = 512
NOPE_DIM = 128
ROPE_DIM = 64
V_DIM = 128
ROPE_THETA = 10000.0
HEAD_PAD = 256

D_FF = 5632
N_EXPERTS = 8
D_FF_EXPERT = 2816

PLE_DIM = 256
DEEPNORM_ALPHA = (2.0 * DEPTH) ** 0.25
LN_EPS = 1e-5
RMS_EPS = 1e-6

VMEM_LIMIT = 56 * 1024 * 1024


def _params(*sem):
    return pltpu.CompilerParams(dimension_semantics=sem, vmem_limit_bytes=VMEM_LIMIT)


def _layer_norm(x, g, b):
    mu = jnp.mean(x, axis=-1, keepdims=True)
    xc = x - mu
    var = jnp.mean(xc * xc, axis=-1, keepdims=True)
    return xc * lax.rsqrt(var + LN_EPS) * g + b


def _rms_norm(x, g):
    return x * lax.rsqrt(jnp.mean(x * x, axis=-1, keepdims=True) + RMS_EPS) * g


def _silu(x):
    return x * (1.0 / (1.0 + jnp.exp(-x)))


def _sigmoid(x):
    return 1.0 / (1.0 + jnp.exp(-x))


def _rope_table_kernel(pos_ref, freq_ref, o_ref):
    ang = pos_ref[...].astype(F32) * freq_ref[...]
    lane = lax.broadcasted_iota(jnp.int32, ang.shape, 1)
    c = jnp.cos(ang)
    s = jnp.sin(ang)
    o_ref[...] = jnp.where(lane < 64, c, jnp.where(lane < 96, -s, s))


def _rope_table(pos_col, freq_row, tm=1024):
    return pl.pallas_call(
        _rope_table_kernel,
        grid=(TOKENS // tm,),
        in_specs=[pl.BlockSpec((tm, 1), lambda i: (i, 0)),
                  pl.BlockSpec((1, 128), lambda i: (0, 0))],
        out_specs=pl.BlockSpec((tm, 128), lambda i: (i, 0)),
        out_shape=jax.ShapeDtypeStruct((TOKENS, 128), F32),
        compiler_params=_params("parallel"),
        name="rope_table",
    )(pos_col, freq_row)


def _matmul_kernel(a_ref, w_ref, o_ref, a_bf):
    @pl.when(pl.program_id(1) == 0)
    def _():
        a_bf[...] = a_ref[...].astype(BF16)

    o_ref[...] = jnp.dot(a_bf[...], w_ref[...], preferred_element_type=F32).astype(o_ref.dtype)


def _matmul(a, w, out_dtype, tm, tn):
    m, k = a.shape
    n = w.shape[1]
    return pl.pallas_call(
        _matmul_kernel,
        grid=(m // tm, n // tn),
        in_specs=[pl.BlockSpec((tm, k), lambda i, j: (i, 0)),
                  pl.BlockSpec((k, tn), lambda i, j: (0, j))],
        out_specs=pl.BlockSpec((tm, tn), lambda i, j: (i, j)),
        out_shape=jax.ShapeDtypeStruct((m, n), out_dtype),
        scratch_shapes=[pltpu.VMEM((tm, k), BF16)],
        compiler_params=_params("parallel", "arbitrary"),
        name="matmul",
    )(a, w)


N_SPLIT = 3
N_WIDE = 4


def _split_bf16(v):
    parts = []
    for _ in range(N_SPLIT):
        part = v.astype(BF16)
        parts.append(part)
        v = v - part.astype(F32)
    return parts


def _dot_split(x, w_hi_ref, w_lo_ref):
    x_hi = x.astype(BF16)
    x_lo = (x - x_hi.astype(F32)).astype(BF16)
    w_hi = w_hi_ref[...]
    return (jnp.dot(x_hi, w_hi, preferred_element_type=F32)
            + jnp.dot(x_lo, w_hi, preferred_element_type=F32)
            + jnp.dot(x_hi, w_lo_ref[...], preferred_element_type=F32))


def _split_weight(w):
    w_hi = w.astype(BF16)
    return w_hi, (w - w_hi.astype(F32)).astype(BF16)


def _dt_kernel(x_ref, w_hi_ref, w_lo_ref, bias_ref, alog_ref, parts_ref, acum_ref, *, tm):
    raw = _dot_split(x_ref[...], w_hi_ref, w_lo_ref) + bias_ref[...]
    dt = jnp.maximum(raw, 0.0) + jnp.log1p(jnp.exp(-jnp.abs(raw)))
    a = -jnp.exp(alog_ref[...]) * dt
    row = lax.broadcasted_iota(jnp.int32, (SSM_CHUNK, SSM_CHUNK), 0)
    col = lax.broadcasted_iota(jnp.int32, (SSM_CHUNK, SSM_CHUNK), 1)
    tril = jnp.where(row >= col, 1.0, 0.0).astype(F32)
    for c in range(tm // SSM_CHUNK):
        sl = slice(c * SSM_CHUNK, (c + 1) * SSM_CHUNK)
        ac = jnp.dot(tril, a[sl, :], preferred_element_type=F32, precision=HIGHEST)
        acum_ref[sl, :] = ac
        ac_end = ac[SSM_CHUNK - 1:SSM_CHUNK, :]
        for qty, val in enumerate((dt[sl, :], jnp.exp(ac), jnp.exp(ac_end - ac), ac)):
            for part, piece in enumerate(_split_bf16(val)):
                parts_ref[qty * N_SPLIT + part, sl, :] = piece


def _dt_prep(x, w_dt, dt_bias, a_log, tm=512):
    n_parts = N_WIDE * N_SPLIT
    w_hi, w_lo = _split_weight(w_dt)
    return pl.pallas_call(
        functools.partial(_dt_kernel, tm=tm),
        grid=(TOKENS // tm,),
        in_specs=[pl.BlockSpec((tm, D_MODEL), lambda i: (i, 0)),
                  pl.BlockSpec((D_MODEL, SSM_HEADS), lambda i: (0, 0)),
                  pl.BlockSpec((D_MODEL, SSM_HEADS), lambda i: (0, 0)),
                  pl.BlockSpec((1, SSM_HEADS), lambda i: (0, 0)),
                  pl.BlockSpec((1, SSM_HEADS), lambda i: (0, 0))],
        out_specs=[pl.BlockSpec((n_parts, tm, SSM_HEADS), lambda i: (0, i, 0)),
                   pl.BlockSpec((tm, SSM_HEADS), lambda i: (i, 0))],
        out_shape=[jax.ShapeDtypeStruct((n_parts, TOKENS, SSM_HEADS), BF16),
                   jax.ShapeDtypeStruct((TOKENS, SSM_HEADS), F32)],
        compiler_params=_params("parallel"),
        name="dt_prep",
    )(x, w_hi, w_lo, dt_bias, a_log)


CONV_ROWS = 256
CONV_HALO = 16


def _conv_kernel(u_ref, w_ref, b_ref, o_ref):
    w = w_ref[...]
    b = b_ref[...]
    tc = u_ref.shape[1]
    for i in range(SEQ // CONV_ROWS):
        r = i * CONV_ROWS
        cur = u_ref[r:r + CONV_ROWS, :].astype(F32)
        if i == 0:
            halo = jnp.zeros((CONV_HALO, tc), F32)
        else:
            halo = u_ref[r - CONV_HALO:r, :].astype(F32)
        win = jnp.concatenate([halo, cur], axis=0)
        acc = b + w[CONV_K - 1:CONV_K, :] * cur
        for k in range(CONV_K - 1):
            shifted = pltpu.roll(win, CONV_K - 1 - k, axis=0)[CONV_HALO:, :]
            acc = acc + w[k:k + 1, :] * shifted
        o_ref[r:r + CONV_ROWS, :] = _silu(acc).astype(o_ref.dtype)


def _conv_silu(zx, conv_w, conv_b, tc=512):
    col0 = D_INNER // tc
    return pl.pallas_call(
        _conv_kernel,
        grid=(BATCH, CONV_DIM // tc),
        in_specs=[pl.BlockSpec((SEQ, tc), lambda b, j: (b, col0 + j)),
                  pl.BlockSpec((CONV_K, tc), lambda b, j: (0, j)),
                  pl.BlockSpec((1, tc), lambda b, j: (0, j))],
        out_specs=pl.BlockSpec((SEQ, tc), lambda b, j: (b, j)),
        out_shape=jax.ShapeDtypeStruct((TOKENS, CONV_DIM), BF16),
        compiler_params=_params("parallel", "parallel"),
        name="conv_silu",
    )(zx, conv_w, conv_b)


WIDE_ROWS = N_WIDE * N_SPLIT * HEADS_PER_GROUP
WIDE_ACUM = 3 * GROUP_WIDTH
WIDE_COLS = WIDE_ACUM + HEADS_PER_GROUP * SSM_CHUNK


def _widen_matrix():
    r = jnp.arange(WIDE_ROWS)[:, None]
    c = jnp.arange(WIDE_COLS)[None, :]
    qty = r // (N_SPLIT * HEADS_PER_GROUP)
    head = r % HEADS_PER_GROUP
    chan = (c // GROUP_WIDTH == qty) & ((c % GROUP_WIDTH) // SSM_HEAD_DIM == head) & (c < WIDE_ACUM)
    copy = (qty == N_WIDE - 1) & (c >= WIDE_ACUM) & ((c - WIDE_ACUM) // SSM_CHUNK == head)
    return (chan | copy).astype(BF16)


def _ssd_kernel(x_ref, b_ref, c_ref, z_ref, parts_ref, acumt_ref, widen_ref, d_ref, g_ref,
                o_ref, state_ref):
    q = SSM_CHUNK
    state_ref[...] = jnp.zeros_like(state_ref)
    trow = lax.broadcasted_iota(jnp.int32, (q, q), 0)
    tcol = lax.broadcasted_iota(jnp.int32, (q, q), 1)
    causal = trow >= tcol
    first_head = lax.broadcasted_iota(jnp.int32, (q, 2 * SSM_HEAD_DIM), 1) < SSM_HEAD_DIM

    def chunk(c, carry):
        rows = pl.ds(pl.multiple_of(c * q, q), q)
        xf = x_ref[rows, :].astype(F32)
        bm = b_ref[rows, :]
        cm = c_ref[rows, :]
        act = acumt_ref[0, 0, c]
        wide = jnp.dot(parts_ref[0, 0, rows, :], widen_ref[...], preferred_element_type=F32)
        xdt = xf * wide[:, :GROUP_WIDTH]
        decay_out = wide[:, GROUP_WIDTH:2 * GROUP_WIDTH]
        to_end = wide[:, 2 * GROUP_WIDTH:WIDE_ACUM]
        xdt_bf = xdt.astype(BF16)

        cb = lax.dot_general(cm, bm, (((1,), (1,)), ((), ())), preferred_element_type=F32)
        state = state_ref[...]
        y = jnp.dot(cm, state.astype(BF16), preferred_element_type=F32) * decay_out

        pairs = []
        for pr in range(HEADS_PER_GROUP // 2):
            xpair = xdt_bf[:, pr * 128:(pr + 1) * 128]
            ms = []
            for j in (2 * pr, 2 * pr + 1):
                col0 = WIDE_ACUM + j * q
                seg = jnp.where(causal, wide[:, col0:col0 + q] - act[j:j + 1, :], -jnp.inf)
                ms.append((cb * jnp.exp(seg)).astype(BF16))
            zero = jnp.zeros_like(xpair)
            rhs = jnp.concatenate([jnp.where(first_head, xpair, zero),
                                   jnp.where(first_head, zero, xpair)], axis=0)
            pairs.append(jnp.dot(jnp.concatenate(ms, axis=1), rhs, preferred_element_type=F32))
        y = y + jnp.concatenate(pairs, axis=1)

        upd = lax.dot_general(bm, (xdt * to_end).astype(BF16), (((0,), (0,)), ((), ())),
                              preferred_element_type=F32)
        state_ref[...] = state * decay_out[q - 1:q, :] + upd

        y = y + d_ref[...] * xf
        y = y * _silu(z_ref[rows, :].astype(F32))
        o_ref[rows, :] = _rms_norm(y, g_ref[...]).astype(o_ref.dtype)
        return carry

    lax.fori_loop(0, N_CHUNKS, chunk, 0, unroll=2)


def _ssd(xbc, zx, parts, acumt, d_full, norm_g):
    x_cols = D_INNER // SSM_STATE
    nb = SSM_GROUPS
    return pl.pallas_call(
        _ssd_kernel,
        grid=(BATCH, SSM_GROUPS),
        in_specs=[
            pl.BlockSpec((SEQ, GROUP_WIDTH), lambda b, g: (b, g)),
            pl.BlockSpec((SEQ, SSM_STATE), lambda b, g: (b, x_cols + g)),
            pl.BlockSpec((SEQ, SSM_STATE), lambda b, g: (b, x_cols + nb + g)),
            pl.BlockSpec((SEQ, GROUP_WIDTH), lambda b, g: (b, g)),
            pl.BlockSpec((1, 1, SEQ, WIDE_ROWS), lambda b, g: (b, g, 0, 0)),
            pl.BlockSpec((1, 1, N_CHUNKS, HEADS_PER_GROUP, SSM_CHUNK), lambda b, g: (b, g, 0, 0, 0)),
            pl.BlockSpec((WIDE_ROWS, WIDE_COLS), lambda b, g: (0, 0)),
            pl.BlockSpec((1, GROUP_WIDTH), lambda b, g: (0, g)),
            pl.BlockSpec((1, GROUP_WIDTH), lambda b, g: (0, g)),
        ],
        out_specs=pl.BlockSpec((SEQ, GROUP_WIDTH), lambda b, g: (b, g)),
        out_shape=jax.ShapeDtypeStruct((TOKENS, D_INNER), BF16),
        scratch_shapes=[pltpu.VMEM((SSM_STATE, GROUP_WIDTH), F32)],
        compiler_params=_params("parallel", "parallel"),
        name="ssd_scan",
    )(xbc, xbc, xbc, zx, parts, acumt, _widen_matrix(), d_full, norm_g)


def _matmul_ln_kernel(a_ref, w_ref, res_ref, g_ref, b_ref, o_ref, *maybe_bf_ref):
    mix = jnp.dot(a_ref[...], w_ref[...], preferred_element_type=F32)
    out = _layer_norm(DEEPNORM_ALPHA * res_ref[...] + mix, g_ref[...], b_ref[...])
    o_ref[...] = out
    for ref in maybe_bf_ref:
        ref[...] = out.astype(BF16)


def _resident(shape):
    return pl.BlockSpec(shape, lambda *_: (0,) * len(shape), pipeline_mode=pl.Buffered(1))


def _matmul_ln(a, w, res, g, b, also_bf16=False, tm=512):
    m, k = a.shape
    n = w.shape[1]
    row = pl.BlockSpec((tm, n), lambda i: (i, 0))
    out_shape = [jax.ShapeDtypeStruct((m, n), F32)]
    if also_bf16:
        out_shape.append(jax.ShapeDtypeStruct((m, n), BF16))
    outs = pl.pallas_call(
        _matmul_ln_kernel,
        grid=(m // tm,),
        in_specs=[pl.BlockSpec((tm, k), lambda i: (i, 0)),
                  _resident((k, n)), row, _resident((1, n)), _resident((1, n))],
        out_specs=[row] * len(out_shape),
        out_shape=out_shape,
        compiler_params=_params("parallel"),
        name="matmul_ln",
    )(a, w, res, g, b)
    return outs if also_bf16 else outs[0]


def _add_ple(h2, p_ref, wpg_ref, wpp_ref):
    gate = jnp.dot(h2.astype(BF16), wpg_ref[...], preferred_element_type=F32)
    proj = jnp.dot(p_ref[...].astype(BF16), wpp_ref[...], preferred_element_type=F32)
    return h2 + _sigmoid(gate) * proj


def _swiglu_ln_kernel(h_ref, wg_ref, wu_ref, wd_ref, g_ref, b_ref, p_ref, wpg_ref, wpp_ref,
                      o_ref, h_bf, acc_ref):
    f = pl.program_id(1)

    @pl.when(f == 0)
    def _():
        h_bf[...] = h_ref[...].astype(BF16)
        acc_ref[...] = jnp.zeros_like(acc_ref)

    a = h_bf[...]
    gate = jnp.dot(a, wg_ref[...], preferred_element_type=F32)
    up = jnp.dot(a, wu_ref[...], preferred_element_type=F32)
    mid = (_silu(gate) * up).astype(BF16)
    acc_ref[...] += jnp.dot(mid, wd_ref[...], preferred_element_type=F32)

    @pl.when(f == pl.num_programs(1) - 1)
    def _():
        x = DEEPNORM_ALPHA * h_ref[...] + acc_ref[...]
        h2 = _layer_norm(x, g_ref[...], b_ref[...])
        o_ref[...] = _add_ple(h2, p_ref, wpg_ref, wpp_ref)


def _swiglu_ln_ple(h, wg, wu, wd, g, b, p, wpg, wpp, tm=512, tf=512):
    m = h.shape[0]
    ff = wg.shape[1]
    return pl.pallas_call(
        _swiglu_ln_kernel,
        grid=(m // tm, ff // tf),
        in_specs=[pl.BlockSpec((tm, D_MODEL), lambda i, f: (i, 0)),
                  pl.BlockSpec((D_MODEL, tf), lambda i, f: (0, f)),
                  pl.BlockSpec((D_MODEL, tf), lambda i, f: (0, f)),
                  pl.BlockSpec((tf, D_MODEL), lambda i, f: (f, 0)),
                  _resident((1, D_MODEL)), _resident((1, D_MODEL)),
                  pl.BlockSpec((tm, PLE_DIM), lambda i, f: (i, 0)),
                  _resident((D_MODEL, D_MODEL)), _resident((PLE_DIM, D_MODEL))],
        out_specs=pl.BlockSpec((tm, D_MODEL), lambda i, f: (i, 0)),
        out_shape=jax.ShapeDtypeStruct((m, D_MODEL), F32),
        scratch_shapes=[pltpu.VMEM((tm, D_MODEL), BF16), pltpu.VMEM((tm, D_MODEL), F32)],
        compiler_params=_params("parallel", "arbitrary"),
        name="swiglu_ln_ple",
    )(h, wg, wu, wd, g, b, p, wpg, wpp)


def _latent_kernel(h_ref, w_ref, gkv_ref, gq_ref, tab_ref, ckv_ref, cq_ref, kr_ref):
    r = jnp.dot(h_ref[...].astype(BF16), w_ref[...], preferred_element_type=F32)
    ckv_ref[...] = _rms_norm(r[:, :KV_RANK], gkv_ref[...]).astype(BF16)
    cq_ref[...] = _rms_norm(r[:, KV_RANK:KV_RANK + Q_RANK], gq_ref[...]).astype(BF16)
    t = r[:, KV_RANK + Q_RANK:] * tab_ref[...]
    rot = t + pltpu.roll(t, ROPE_DIM, axis=1)
    lane = lax.broadcasted_iota(jnp.int32, rot.shape, 1)
    kr_ref[...] = jnp.where(lane < ROPE_DIM, rot, 0.0).astype(BF16)


def _latent(h, w_cat, g_kv, g_q, table, tm=512):
    n = w_cat.shape[1]
    return pl.pallas_call(
        _latent_kernel,
        grid=(TOKENS // tm,),
        in_specs=[pl.BlockSpec((tm, D_MODEL), lambda i: (i, 0)),
                  pl.BlockSpec((D_MODEL, n), lambda i: (0, 0)),
                  pl.BlockSpec((1, KV_RANK), lambda i: (0, 0)),
                  pl.BlockSpec((1, Q_RANK), lambda i: (0, 0)),
                  pl.BlockSpec((tm, 128), lambda i: (i, 0))],
        out_specs=[pl.BlockSpec((tm, KV_RANK), lambda i: (i, 0)),
                   pl.BlockSpec((tm, Q_RANK), lambda i: (i, 0)),
                   pl.BlockSpec((tm, 128), lambda i: (i, 0))],
        out_shape=[jax.ShapeDtypeStruct((TOKENS, KV_RANK), BF16),
                   jax.ShapeDtypeStruct((TOKENS, Q_RANK), BF16),
                   jax.ShapeDtypeStruct((TOKENS, 128), BF16)],
        compiler_params=_params("parallel"),
        name="latent",
    )(h, w_cat, g_kv, g_q, table)


def _kv_up_kernel(c_ref, wk_ref, wv_ref, kr_ref, k_ref, v_ref):
    c = c_ref[...]
    kn = jnp.dot(c, wk_ref[...], preferred_element_type=F32).astype(BF16)
    v_ref[...] = jnp.dot(c, wv_ref[...], preferred_element_type=F32).astype(BF16)
    kr = kr_ref[...]
    for h in range(MLA_HEADS):
        k_ref[:, h * HEAD_PAD:h * HEAD_PAD + NOPE_DIM] = kn[:, h * NOPE_DIM:(h + 1) * NOPE_DIM]
        k_ref[:, h * HEAD_PAD + NOPE_DIM:(h + 1) * HEAD_PAD] = kr


def _kv_up(c_kv, w_uk, w_uv, k_rope, tm=512):
    return pl.pallas_call(
        _kv_up_kernel,
        grid=(TOKENS // tm,),
        in_specs=[pl.BlockSpec((tm, KV_RANK), lambda i: (i, 0)),
                  pl.BlockSpec((KV_RANK, MLA_HEADS * NOPE_DIM), lambda i: (0, 0)),
                  pl.BlockSpec((KV_RANK, MLA_HEADS * V_DIM), lambda i: (0, 0)),
                  pl.BlockSpec((tm, 128), lambda i: (i, 0))],
        out_specs=[pl.BlockSpec((tm, MLA_HEADS * HEAD_PAD), lambda i: (i, 0)),
                   pl.BlockSpec((tm, MLA_HEADS * V_DIM), lambda i: (i, 0))],
        out_shape=[jax.ShapeDtypeStruct((TOKENS, MLA_HEADS * HEAD_PAD), BF16),
                   jax.ShapeDtypeStruct((TOKENS, MLA_HEADS * V_DIM), BF16)],
        compiler_params=_params("parallel"),
        name="kv_up",
    )(c_kv, w_uk, w_uv, k_rope)


def _q_up_kernel(c_ref, w_ref, tab_ref, q_ref):
    scale = (NOPE_DIM + ROPE_DIM) ** -0.5
    r = jnp.dot(c_ref[...], w_ref[...], preferred_element_type=F32)
    tab = tab_ref[...]
    for h in range(MLA_HEADS):
        base = h * HEAD_PAD
        q_ref[:, base:base + NOPE_DIM] = (r[:, base:base + NOPE_DIM] * scale).astype(BF16)
        t = r[:, base + NOPE_DIM:base + HEAD_PAD] * tab
        rot = (t + pltpu.roll(t, ROPE_DIM, axis=1)) * scale
        q_ref[:, base + NOPE_DIM:base + HEAD_PAD] = rot.astype(BF16)


def _q_up(c_q, w_q, table, tm=512):
    return pl.pallas_call(
        _q_up_kernel,
        grid=(TOKENS // tm,),
        in_specs=[pl.BlockSpec((tm, Q_RANK), lambda i: (i, 0)),
                  pl.BlockSpec((Q_RANK, MLA_HEADS * HEAD_PAD), lambda i: (0, 0)),
                  pl.BlockSpec((tm, 128), lambda i: (i, 0))],
        out_specs=pl.BlockSpec((tm, MLA_HEADS * HEAD_PAD), lambda i: (i, 0)),
        out_shape=jax.ShapeDtypeStruct((TOKENS, MLA_HEADS * HEAD_PAD), BF16),
        compiler_params=_params("parallel"),
        name="q_up",
    )(c_q, w_q, table)


ATT_TQ = 512
ATT_TK = 512


def _attn_kernel(q_ref, k_ref, v_ref, o_ref):
    row = lax.broadcasted_iota(jnp.int32, (ATT_TQ, ATT_TK), 0)
    col = lax.broadcasted_iota(jnp.int32, (ATT_TQ, ATT_TK), 1)
    causal = row >= col
    for qi in range(SEQ // ATT_TQ):
        q = q_ref[qi * ATT_TQ:(qi + 1) * ATT_TQ, :]
        m = l = acc = None
        for kj in range(qi + 1):
            k = k_ref[kj * ATT_TK:(kj + 1) * ATT_TK, :]
            v = v_ref[kj * ATT_TK:(kj + 1) * ATT_TK, :]
            s = lax.dot_general(q, k, (((1,), (1,)), ((), ())), preferred_element_type=F32)
            if kj == qi:
                s = jnp.where(causal, s, -jnp.inf)
            s_max = jnp.max(s, axis=-1, keepdims=True)
            if kj == 0:
                m = s_max
                p = jnp.exp(s - m)
                l = jnp.sum(p, axis=-1, keepdims=True)
                acc = jnp.dot(p.astype(BF16), v, preferred_element_type=F32)
            else:
                m_new = jnp.maximum(m, s_max)
                alpha = jnp.exp(m - m_new)
                p = jnp.exp(s - m_new)
                l = alpha * l + jnp.sum(p, axis=-1, keepdims=True)
                acc = alpha * acc + jnp.dot(p.astype(BF16), v, preferred_element_type=F32)
                m = m_new
        o_ref[qi * ATT_TQ:(qi + 1) * ATT_TQ, :] = (acc / l).astype(o_ref.dtype)


def _attention(q_cat, k_cat, v):
    return pl.pallas_call(
        _attn_kernel,
        grid=(BATCH, MLA_HEADS),
        in_specs=[pl.BlockSpec((SEQ, HEAD_PAD), lambda b, h: (b, h)),
                  pl.BlockSpec((SEQ, HEAD_PAD), lambda b, h: (b, h)),
                  pl.BlockSpec((SEQ, V_DIM), lambda b, h: (b, h))],
        out_specs=pl.BlockSpec((SEQ, V_DIM), lambda b, h: (b, h)),
        out_shape=jax.ShapeDtypeStruct((TOKENS, MLA_HEADS * V_DIM), BF16),
        compiler_params=_params("parallel", "parallel"),
        name="attention",
    )(q_cat, k_cat, v)


def _router_kernel(h_ref, w_hi_ref, w_lo_ref, b_ref, tri_ref, o_ref, cnt_ref, carry_ref):
    @pl.when(pl.program_id(0) == 0)
    def _():
        carry_ref[...] = jnp.zeros_like(carry_ref)

    logits = _dot_split(h_ref[...], w_hi_ref, w_lo_ref) + b_ref[...]
    lane = lax.broadcasted_iota(jnp.int32, logits.shape, 1).astype(F32)
    logits = jnp.where(lane < N_EXPERTS, logits, -jnp.inf)
    m1 = jnp.max(logits, axis=-1, keepdims=True)
    i1 = jnp.min(jnp.where(logits == m1, lane, 128.0), axis=-1, keepdims=True)
    rest = jnp.where(lane == i1, -jnp.inf, logits)
    m2 = jnp.max(rest, axis=-1, keepdims=True)
    i2 = jnp.min(jnp.where(rest == m2, lane, 128.0), axis=-1, keepdims=True)
    e2 = jnp.exp(m2 - m1)
    w1 = 1.0 / (1.0 + e2)
    w2 = e2 / (1.0 + e2)
    first = lane == i1
    second = lane == i2
    chosen = jnp.where(jnp.logical_or(first, second), 1.0, 0.0)
    before = carry_ref[0:1, :] + jnp.dot(tri_ref[...], chosen.astype(BF16),
                                         preferred_element_type=F32)
    r1 = jnp.sum(jnp.where(first, before, 0.0), axis=-1, keepdims=True)
    r2 = jnp.sum(jnp.where(second, before, 0.0), axis=-1, keepdims=True)
    total = carry_ref[0:1, :] + jnp.sum(chosen, axis=0, keepdims=True)
    carry_ref[...] = jnp.broadcast_to(total, carry_ref.shape)
    cnt_ref[...] = jnp.broadcast_to(total, cnt_ref.shape)
    out = jnp.where(lane == 0.0, i1, jnp.where(lane == 1.0, i2, 0.0))
    out = jnp.where(lane == 2.0, w1, jnp.where(lane == 3.0, w2, out))
    out = jnp.where(lane == 4.0, r1, jnp.where(lane == 5.0, r2, out))
    o_ref[...] = out


def _router(h, w_router, b_router, tm=512):
    w_hi, w_lo = _split_weight(jnp.pad(w_router, ((0, 0), (0, 128 - N_EXPERTS))))
    b_pad = jnp.pad(b_router, ((0, 0), (0, 128 - N_EXPERTS)))
    tri = jnp.tril(jnp.ones((tm, tm), BF16), k=-1)
    return pl.pallas_call(
        _router_kernel,
        grid=(TOKENS // tm,),
        in_specs=[pl.BlockSpec((tm, D_MODEL), lambda i: (i, 0)),
                  pl.BlockSpec((D_MODEL, 128), lambda i: (0, 0)),
                  pl.BlockSpec((D_MODEL, 128), lambda i: (0, 0)),
                  pl.BlockSpec((1, 128), lambda i: (0, 0)),
                  pl.BlockSpec((tm, tm), lambda i: (0, 0))],
        out_specs=[pl.BlockSpec((tm, 128), lambda i: (i, 0)),
                   pl.BlockSpec((8, 128), lambda i: (0, 0))],
        out_shape=[jax.ShapeDtypeStruct((TOKENS, 128), F32),
                   jax.ShapeDtypeStruct((8, 128), F32)],
        scratch_shapes=[pltpu.VMEM((8, 128), F32)],
        compiler_params=_params("arbitrary"),
        name="router",
    )(h, w_hi, w_lo, b_pad, tri)


MOE_TM = 1024
MOE_TF = 256
MOE_ROWS = 2 * TOKENS + N_EXPERTS * MOE_TM
MOE_TILES = MOE_ROWS // MOE_TM


def _moe_kernel(te_ref, nt_ref, x_ref, wg_ref, wu_ref, wd_ref, o_ref, acc_ref):
    i = pl.program_id(0)
    f = pl.program_id(1)
    last = pl.num_programs(1) - 1
    valid = i < nt_ref[0]

    @pl.when(valid)
    def _():
        @pl.when(f == 0)
        def _():
            acc_ref[...] = jnp.zeros_like(acc_ref)

        a = x_ref[...]
        gate = jnp.dot(a, wg_ref[0].astype(BF16), preferred_element_type=F32)
        up = jnp.dot(a, wu_ref[0].astype(BF16), preferred_element_type=F32)
        mid = (_silu(gate) * up).astype(BF16)
        acc_ref[...] += jnp.dot(mid, wd_ref[0].astype(BF16), preferred_element_type=F32)

        @pl.when(f == last)
        def _():
            o_ref[...] = acc_ref[...].astype(o_ref.dtype)

    @pl.when(jnp.logical_and(jnp.logical_not(valid), f == last))
    def _():
        o_ref[...] = jnp.zeros_like(o_ref)


def _moe_experts(tile_expert, n_tiles, x_sorted, wg, wu, wd):
    nf = D_FF_EXPERT // MOE_TF

    def row_map(i, f, te, nt):
        return (jnp.minimum(i, nt[0] - 1), 0)

    def f_of(i, f, nt):
        return jnp.where(i < nt[0], f, nf - 1)

    grid_spec = pltpu.PrefetchScalarGridSpec(
        num_scalar_prefetch=2,
        grid=(MOE_TILES, nf),
        in_specs=[
            pl.BlockSpec((MOE_TM, D_MODEL), row_map),
            pl.BlockSpec((1, D_MODEL, MOE_TF), lambda i, f, te, nt: (te[i], 0, f_of(i, f, nt))),
            pl.BlockSpec((1, D_MODEL, MOE_TF), lambda i, f, te, nt: (te[i], 0, f_of(i, f, nt))),
            pl.BlockSpec((1, MOE_TF, D_MODEL), lambda i, f, te, nt: (te[i], f_of(i, f, nt), 0)),
        ],
        out_specs=pl.BlockSpec((MOE_TM, D_MODEL), lambda i, f, te, nt: (i, 0)),
        scratch_shapes=[pltpu.VMEM((MOE_TM, D_MODEL), F32)],
    )
    return pl.pallas_call(
        _moe_kernel,
        grid_spec=grid_spec,
        out_shape=jax.ShapeDtypeStruct((MOE_ROWS, D_MODEL), BF16),
        compiler_params=_params("arbitrary", "arbitrary"),
        name="moe_experts",
    )(tile_expert, n_tiles, x_sorted, wg, wu, wd)


def _combine_ln_kernel(h_ref, y0_ref, y1_ref, w0_ref, w1_ref, g_ref, b_ref, p_ref, wpg_ref,
                       wpp_ref, o_ref):
    ff = w0_ref[...] * y0_ref[...].astype(F32) + w1_ref[...] * y1_ref[...].astype(F32)
    h2 = _layer_norm(DEEPNORM_ALPHA * h_ref[...] + ff, g_ref[...], b_ref[...])
    o_ref[...] = _add_ple(h2, p_ref, wpg_ref, wpp_ref)


def _combine_ln_ple(h, y0, y1, w0, w1, g, b, p, wpg, wpp, tm=512):
    row = pl.BlockSpec((tm, D_MODEL), lambda i: (i, 0))
    col = pl.BlockSpec((tm, 1), lambda i: (i, 0))
    return pl.pallas_call(
        _combine_ln_kernel,
        grid=(TOKENS // tm,),
        in_specs=[row, row, row, col, col, _resident((1, D_MODEL)), _resident((1, D_MODEL)),
                  pl.BlockSpec((tm, PLE_DIM), lambda i: (i, 0)),
                  _resident((D_MODEL, D_MODEL)), _resident((PLE_DIM, D_MODEL))],
        out_specs=row,
        out_shape=jax.ShapeDtypeStruct((TOKENS, D_MODEL), F32),
        compiler_params=_params("parallel"),
        name="combine_ln_ple",
    )(h, y0, y1, w0, w1, g, b, p, wpg, wpp)


def _swap_halves(w):
    half = w.shape[-1] // 2
    return jnp.concatenate([w[..., half:], w[..., :half]], axis=-1)


def kernel(x, p, positions, ssm_w_in, ssm_conv_w, ssm_conv_b, ssm_dt_bias, ssm_a_log, ssm_d, ssm_norm_g, ssm_w_out, kv_w_down, kv_norm_g, kv_w_rope, kv_w_uk, kv_w_uv, mla_w_dq, mla_q_norm_g, mla_w_uq, mla_w_o, ffn_w_gate, ffn_w_up, ffn_w_down, moe_w_router, moe_b_router, moe_w_gate, moe_w_up, moe_w_down, ln1_g, ln1_b, ln2_g, ln2_b, ple_w_proj, ple_w_gate):
    h0 = x.reshape(TOKENS, D_MODEL)
    p = p.reshape(DEPTH, TOKENS, PLE_DIM)
    row = lambda v: v.reshape(1, -1)

    zx_cols = D_INNER + CONV_DIM
    w_zx = ssm_w_in[0][:, :zx_cols].astype(BF16)
    w_dt = ssm_w_in[0][:, zx_cols:]
    zx = _matmul(h0, w_zx, BF16, tm=1024, tn=1024)
    parts, acum = _dt_prep(h0, w_dt, row(ssm_dt_bias[0]), row(ssm_a_log[0]))
    xbc = _conv_silu(zx, ssm_conv_w[0], row(ssm_conv_b[0]))
    parts = parts.reshape(N_WIDE * N_SPLIT, BATCH, SEQ, SSM_GROUPS, HEADS_PER_GROUP)
    parts = parts.transpose(1, 3, 2, 0, 4).reshape(BATCH, SSM_GROUPS, SEQ, WIDE_ROWS)
    acumt = acum.reshape(BATCH, N_CHUNKS, SSM_CHUNK, SSM_GROUPS, HEADS_PER_GROUP)
    acumt = acumt.transpose(0, 3, 1, 4, 2)
    d_full = row(jnp.repeat(ssm_d[0], SSM_HEAD_DIM))
    y = _ssd(xbc, zx, parts, acumt, d_full, row(ssm_norm_g[0]))
    h = _matmul_ln(y, ssm_w_out[0].astype(BF16), h0, row(ln1_g[0]), row(ln1_b[0]))

    h = _swiglu_ln_ple(h, ffn_w_gate[0].astype(BF16), ffn_w_up[0].astype(BF16),
                       ffn_w_down[0].astype(BF16), row(ln2_g[0]), row(ln2_b[0]),
                       p[0], ple_w_gate[0].astype(BF16), ple_w_proj[0].astype(BF16))

    inv_freq = ROPE_THETA ** (-jnp.arange(0, ROPE_DIM, 2, dtype=F32) / ROPE_DIM)
    table = _rope_table(positions.reshape(TOKENS, 1), row(jnp.tile(inv_freq, 4)))
    w_lat = jnp.concatenate([kv_w_down, mla_w_dq[0], kv_w_rope, _swap_halves(kv_w_rope)],
                            axis=1).astype(BF16)
    c_kv, c_q, k_rope = _latent(h, w_lat, row(kv_norm_g), row(mla_q_norm_g[0]), table)
    k_cat, v = _kv_up(c_kv, kv_w_uk.astype(BF16), kv_w_uv.astype(BF16), k_rope)
    w_uq = mla_w_uq[0].reshape(Q_RANK, MLA_HEADS, NOPE_DIM + ROPE_DIM)
    w_q = jnp.concatenate([w_uq, _swap_halves(w_uq[..., NOPE_DIM:])], axis=-1)
    w_q = w_q.reshape(Q_RANK, MLA_HEADS * HEAD_PAD).astype(BF16)
    q_cat = _q_up(c_q, w_q, table)
    o = _attention(q_cat, k_cat, v)
    h, h_bf = _matmul_ln(o, mla_w_o[0].astype(BF16), h, row(ln1_g[1]), row(ln1_b[1]),
                         also_bf16=True)

    route, counts = _router(h, moe_w_router[0], row(moe_b_router[0]))
    counts = counts[0, :N_EXPERTS].astype(jnp.int32)
    tiles_per = (counts + MOE_TM - 1) // MOE_TM
    tile_end = jnp.cumsum(tiles_per)
    start = (tile_end - tiles_per) * MOE_TM
    expert = route[:, 0:2].astype(jnp.int32)
    rank = route[:, 4:6].astype(jnp.int32)
    start_of = jnp.sum(jnp.where(expert[..., None] == jnp.arange(N_EXPERTS, dtype=jnp.int32),
                                 start, 0), axis=-1)
    pos = start_of + rank
    n_tiles = tile_end[-1:].astype(jnp.int32)
    tile_ids = jnp.minimum(jnp.arange(MOE_TILES, dtype=jnp.int32), n_tiles[0] - 1)
    tile_expert = jnp.searchsorted(tile_end, tile_ids, side="right").astype(jnp.int32)
    token = jnp.broadcast_to(jnp.arange(TOKENS, dtype=jnp.int32)[:, None], (TOKENS, 2))
    src = jnp.zeros((MOE_ROWS,), jnp.int32).at[pos.reshape(-1)].set(
        token.reshape(-1), unique_indices=True)
    x_sorted = jnp.take(h_bf, src, axis=0)
    y_sorted = _moe_experts(tile_expert, n_tiles, x_sorted, moe_w_gate[0], moe_w_up[0],
                            moe_w_down[0])
    y0 = jnp.take(y_sorted, pos[:, 0], axis=0)
    y1 = jnp.take(y_sorted, pos[:, 1], axis=0)
    h = _combine_ln_ple(h, y0, y1, route[:, 2:3], route[:, 3:4], row(ln2_g[1]), row(ln2_b[1]),
                        p[1], ple_w_gate[1].astype(BF16), ple_w_proj[1].astype(BF16))
    return h.reshape(BATCH, SEQ, D_MODEL)
```

```python
import functools

import jax
import jax.numpy as jnp
from jax import lax
from jax.experimental import pallas as pl
from jax.experimental.pallas import tpu as pltpu

F32 = jnp.float32
BF16 = jnp.bfloat16
HIGHEST = lax.Precision.HIGHEST

D_MODEL = 2048
BATCH = 16
SEQ = 2048
TOKENS = BATCH * SEQ
DEPTH = 2

D_INNER = 4096
SSM_HEAD_DIM = 64
SSM_HEADS = 64
SSM_GROUPS = 8
HEADS_PER_GROUP = 8
SSM_STATE = 128
GROUP_WIDTH = HEADS_PER_GROUP * SSM_HEAD_DIM
CONV_K = 4
CONV_DIM = D_INNER + 2 * SSM_GROUPS * SSM_STATE
SSM_CHUNK = 128
N_CHUNKS = SEQ // SSM_CHUNK

MLA_HEADS = 16
Q_RANK = 512
KV_RANK = 512
NOPE_DIM = 128
ROPE_DIM = 64
V_DIM = 128
ROPE_THETA = 10000.0
HEAD_PAD = 256

D_FF = 5632
N_EXPERTS = 8
D_FF_EXPERT = 2816

PLE_DIM = 256
DEEPNORM_ALPHA = (2.0 * DEPTH) ** 0.25
LN_EPS = 1e-5
RMS_EPS = 1e-6

VMEM_LIMIT = 56 * 1024 * 1024


def _params(*sem):
    return pltpu.CompilerParams(dimension_semantics=sem, vmem_limit_bytes=VMEM_LIMIT)


def _layer_norm(x, g, b):
    mu = jnp.mean(x, axis=-1, keepdims=True)
    xc = x - mu
    var = jnp.mean(xc * xc, axis=-1, keepdims=True)
    return xc * lax.rsqrt(var + LN_EPS) * g + b


def _rms_norm(x, g):
    return x * lax.rsqrt(jnp.mean(x * x, axis=-1, keepdims=True) + RMS_EPS) * g


def _silu(x):
    return x * (1.0 / (1.0 + jnp.exp(-x)))


def _sigmoid(x):
    return 1.0 / (1.0 + jnp.exp(-x))


def _rope_table_kernel(pos_ref, freq_ref, o_ref):
    ang = pos_ref[...].astype(F32) * freq_ref[...]
    lane = lax.broadcasted_iota(jnp.int32, ang.shape, 1)
    c = jnp.cos(ang)
    s = jnp.sin(ang)
    o_ref[...] = jnp.where(lane < 64, c, jnp.where(lane < 96, -s, s))


def _rope_table(pos_col, freq_row, tm=1024):
    return pl.pallas_call(
        _rope_table_kernel,
        grid=(TOKENS // tm,),
        in_specs=[pl.BlockSpec((tm, 1), lambda i: (i, 0)),
                  pl.BlockSpec((1, 128), lambda i: (0, 0))],
        out_specs=pl.BlockSpec((tm, 128), lambda i: (i, 0)),
        out_shape=jax.ShapeDtypeStruct((TOKENS, 128), F32),
        compiler_params=_params("parallel"),
        name="rope_table",
    )(pos_col, freq_row)


def _matmul_kernel(a_ref, w_ref, o_ref, a_bf):
    @pl.when(pl.program_id(1) == 0)
    def _():
        a_bf[...] = a_ref[...].astype(BF16)

    o_ref[...] = jnp.dot(a_bf[...], w_ref[...], preferred_element_type=F32).astype(o_ref.dtype)


def _matmul(a, w, n, out_dtype, tm, tn):
    m, k = a.shape
    return pl.pallas_call(
        _matmul_kernel,
        grid=(m // tm, n // tn),
        in_specs=[pl.BlockSpec((tm, k), lambda i, j: (i, 0)),
                  pl.BlockSpec((k, tn), lambda i, j: (0, j))],
        out_specs=pl.BlockSpec((tm, tn), lambda i, j: (i, j)),
        out_shape=jax.ShapeDtypeStruct((m, n), out_dtype),
        scratch_shapes=[pltpu.VMEM((tm, k), BF16)],
        compiler_params=_params("parallel", "arbitrary"),
        name="matmul",
    )(a, w)


N_SPLIT = 3
N_WIDE = 4


def _split_bf16(v):
    parts = []
    for _ in range(N_SPLIT):
        part = v.astype(BF16)
        parts.append(part)
        v = v - part.astype(F32)
    return parts


def _dot_split(x, w_hi_ref, w_lo_ref):
    x_hi = x.astype(BF16)
    x_lo = (x - x_hi.astype(F32)).astype(BF16)
    w_hi = w_hi_ref[...]
    return (jnp.dot(x_hi, w_hi, preferred_element_type=F32)
            + jnp.dot(x_lo, w_hi, preferred_element_type=F32)
            + jnp.dot(x_hi, w_lo_ref[...], preferred_element_type=F32))


def _split_weight(w):
    w_hi = w.astype(BF16)
    return w_hi, (w - w_hi.astype(F32)).astype(BF16)


def _dt_kernel(x_ref, w_hi_ref, w_lo_ref, bias_ref, alog_ref, parts_ref, acum_ref, *, tm):
    raw = _dot_split(x_ref[...], w_hi_ref, w_lo_ref) + bias_ref[...]
    dt = jnp.maximum(raw, 0.0) + jnp.log1p(jnp.exp(-jnp.abs(raw)))
    a = -jnp.exp(alog_ref[...]) * dt
    row = lax.broadcasted_iota(jnp.int32, (SSM_CHUNK, SSM_CHUNK), 0)
    col = lax.broadcasted_iota(jnp.int32, (SSM_CHUNK, SSM_CHUNK), 1)
    tril = jnp.where(row >= col, 1.0, 0.0).astype(F32)
    for c in range(tm // SSM_CHUNK):
        sl = slice(c * SSM_CHUNK, (c + 1) * SSM_CHUNK)
        ac = jnp.dot(tril, a[sl, :], preferred_element_type=F32, precision=HIGHEST)
        acum_ref[sl, :] = ac
        ac_end = ac[SSM_CHUNK - 1:SSM_CHUNK, :]
        for qty, val in enumerate((dt[sl, :], jnp.exp(ac), jnp.exp(ac_end - ac), ac)):
            for part, piece in enumerate(_split_bf16(val)):
                parts_ref[qty * N_SPLIT + part, sl, :] = piece


def _dt_prep(x, w_dt, dt_bias, a_log, tm=512):
    n_parts = N_WIDE * N_SPLIT
    w_hi, w_lo = _split_weight(w_dt)
    return pl.pallas_call(
        functools.partial(_dt_kernel, tm=tm),
        grid=(TOKENS // tm,),
        in_specs=[pl.BlockSpec((tm, D_MODEL), lambda i: (i, 0)),
                  pl.BlockSpec((D_MODEL, SSM_HEADS), lambda i: (0, 0)),
                  pl.BlockSpec((D_MODEL, SSM_HEADS), lambda i: (0, 0)),
                  pl.BlockSpec((1, SSM_HEADS), lambda i: (0, 0)),
                  pl.BlockSpec((1, SSM_HEADS), lambda i: (0, 0))],
        out_specs=[pl.BlockSpec((n_parts, tm, SSM_HEADS), lambda i: (0, i, 0)),
                   pl.BlockSpec((tm, SSM_HEADS), lambda i: (i, 0))],
        out_shape=[jax.ShapeDtypeStruct((n_parts, TOKENS, SSM_HEADS), BF16),
                   jax.ShapeDtypeStruct((TOKENS, SSM_HEADS), F32)],
        compiler_params=_params("parallel"),
        name="dt_prep",
    )(x, w_hi, w_lo, dt_bias, a_log)


CONV_ROWS = 256
CONV_HALO = 16


def _conv_kernel(u_ref, w_ref, b_ref, o_ref):
    w = w_ref[...]
    b = b_ref[...]
    tc = u_ref.shape[1]
    for i in range(SEQ // CONV_ROWS):
        r = i * CONV_ROWS
        cur = u_ref[r:r + CONV_ROWS, :].astype(F32)
        if i == 0:
            halo = jnp.zeros((CONV_HALO, tc), F32)
        else:
            halo = u_ref[r - CONV_HALO:r, :].astype(F32)
        win = jnp.concatenate([halo, cur], axis=0)
        acc = b + w[CONV_K - 1:CONV_K, :] * cur
        for k in range(CONV_K - 1):
            shifted = pltpu.roll(win, CONV_K - 1 - k, axis=0)[CONV_HALO:, :]
            acc = acc + w[k:k + 1, :] * shifted
        o_ref[r:r + CONV_ROWS, :] = _silu(acc).astype(o_ref.dtype)


def _conv_silu(zx, conv_w, conv_b, tc=512):
    col0 = D_INNER // tc
    return pl.pallas_call(
        _conv_kernel,
        grid=(BATCH, CONV_DIM // tc),
        in_specs=[pl.BlockSpec((SEQ, tc), lambda b, j: (b, col0 + j)),
                  pl.BlockSpec((CONV_K, tc), lambda b, j: (0, j)),
                  pl.BlockSpec((1, tc), lambda b, j: (0, j))],
        out_specs=pl.BlockSpec((SEQ, tc), lambda b, j: (b, j)),
        out_shape=jax.ShapeDtypeStruct((TOKENS, CONV_DIM), BF16),
        compiler_params=_params("parallel", "parallel"),
        name="conv_silu",
    )(zx, conv_w, conv_b)


WIDE_ROWS = N_WIDE * N_SPLIT * HEADS_PER_GROUP
WIDE_ACUM = 3 * GROUP_WIDTH
WIDE_COLS = WIDE_ACUM + HEADS_PER_GROUP * SSM_CHUNK


def _widen_matrix():
    r = jnp.arange(WIDE_ROWS)[:, None]
    c = jnp.arange(WIDE_COLS)[None, :]
    qty = r // (N_SPLIT * HEADS_PER_GROUP)
    head = r % HEADS_PER_GROUP
    chan = (c // GROUP_WIDTH == qty) & ((c % GROUP_WIDTH) // SSM_HEAD_DIM == head) & (c < WIDE_ACUM)
    copy = (qty == N_WIDE - 1) & (c >= WIDE_ACUM) & ((c - WIDE_ACUM) // SSM_CHUNK == head)
    return (chan | copy).astype(BF16)


def _ssd_kernel(x_ref, b_ref, c_ref, z_ref, parts_ref, acumt_ref, widen_ref, d_ref, g_ref,
                o_ref, state_ref):
    q = SSM_CHUNK
    state_ref[...] = jnp.zeros_like(state_ref)
    trow = lax.broadcasted_iota(jnp.int32, (q, q), 0)
    tcol = lax.broadcasted_iota(jnp.int32, (q, q), 1)
    causal = trow >= tcol
    first_head = lax.broadcasted_iota(jnp.int32, (q, 2 * SSM_HEAD_DIM), 1) < SSM_HEAD_DIM

    def chunk(c, carry):
        rows = pl.ds(pl.multiple_of(c * q, q), q)
        xf = x_ref[rows, :].astype(F32)
        bm = b_ref[rows, :]
        cm = c_ref[rows, :]
        act = acumt_ref[0, 0, c]
        wide = jnp.dot(parts_ref[0, 0, rows, :], widen_ref[...], preferred_element_type=F32)
        xdt = xf * wide[:, :GROUP_WIDTH]
        decay_out = wide[:, GROUP_WIDTH:2 * GROUP_WIDTH]
        to_end = wide[:, 2 * GROUP_WIDTH:WIDE_ACUM]
        xdt_bf = xdt.astype(BF16)

        cb = lax.dot_general(cm, bm, (((1,), (1,)), ((), ())), preferred_element_type=F32)
        state = state_ref[...]
        y = jnp.dot(cm, state.astype(BF16), preferred_element_type=F32) * decay_out

        pairs = []
        for pr in range(HEADS_PER_GROUP // 2):
            xpair = xdt_bf[:, pr * 128:(pr + 1) * 128]
            ms = []
            for j in (2 * pr, 2 * pr + 1):
                col0 = WIDE_ACUM + j * q
                seg = jnp.where(causal, wide[:, col0:col0 + q] - act[j:j + 1, :], -jnp.inf)
                ms.append((cb * jnp.exp(seg)).astype(BF16))
            zero = jnp.zeros_like(xpair)
            rhs = jnp.concatenate([jnp.where(first_head, xpair, zero),
                                   jnp.where(first_head, zero, xpair)], axis=0)
            pairs.append(jnp.dot(jnp.concatenate(ms, axis=1), rhs, preferred_element_type=F32))
        y = y + jnp.concatenate(pairs, axis=1)

        upd = lax.dot_general(bm, (xdt * to_end).astype(BF16), (((0,), (0,)), ((), ())),
                              preferred_element_type=F32)
        state_ref[...] = state * decay_out[q - 1:q, :] + upd

        y = y + d_ref[...] * xf
        y = y * _silu(z_ref[rows, :].astype(F32))
        o_ref[rows, :] = _rms_norm(y, g_ref[...]).astype(o_ref.dtype)
        return carry

    lax.fori_loop(0, N_CHUNKS, chunk, 0, unroll=2)


def _ssd(xbc, zx, parts, acumt, d_full, norm_g):
    x_cols = D_INNER // SSM_STATE
    nb = SSM_GROUPS
    return pl.pallas_call(
        _ssd_kernel,
        grid=(BATCH, SSM_GROUPS),
        in_specs=[
            pl.BlockSpec((SEQ, GROUP_WIDTH), lambda b, g: (b, g)),
            pl.BlockSpec((SEQ, SSM_STATE), lambda b, g: (b, x_cols + g)),
            pl.BlockSpec((SEQ, SSM_STATE), lambda b, g: (b, x_cols + nb + g)),
            pl.BlockSpec((SEQ, GROUP_WIDTH), lambda b, g: (b, g)),
            pl.BlockSpec((1, 1, SEQ, WIDE_ROWS), lambda b, g: (b, g, 0, 0)),
            pl.BlockSpec((1, 1, N_CHUNKS, HEADS_PER_GROUP, SSM_CHUNK), lambda b, g: (b, g, 0, 0, 0)),
            pl.BlockSpec((WIDE_ROWS, WIDE_COLS), lambda b, g: (0, 0)),
            pl.BlockSpec((1, GROUP_WIDTH), lambda b, g: (0, g)),
            pl.BlockSpec((1, GROUP_WIDTH), lambda b, g: (0, g)),
        ],
        out_specs=pl.BlockSpec((SEQ, GROUP_WIDTH), lambda b, g: (b, g)),
        out_shape=jax.ShapeDtypeStruct((TOKENS, D_INNER), BF16),
        scratch_shapes=[pltpu.VMEM((SSM_STATE, GROUP_WIDTH), F32)],
        compiler_params=_params("parallel", "parallel"),
        name="ssd_scan",
    )(xbc, xbc, xbc, zx, parts, acumt, _widen_matrix(), d_full, norm_g)


def _matmul_ln_kernel(a_ref, w_ref, res_ref, g_ref, b_ref, o_ref):
    mix = jnp.dot(a_ref[...], w_ref[...], preferred_element_type=F32)
    o_ref[...] = _layer_norm(DEEPNORM_ALPHA * res_ref[...] + mix, g_ref[...], b_ref[...])


def _resident(shape):
    return pl.BlockSpec(shape, lambda *_: (0,) * len(shape), pipeline_mode=pl.Buffered(1))


def _matmul_ln(a, w, res, g, b, tm=512):
    m, k = a.shape
    n = w.shape[1]
    row = pl.BlockSpec((tm, n), lambda i: (i, 0))
    return pl.pallas_call(
        _matmul_ln_kernel,
        grid=(m // tm,),
        in_specs=[pl.BlockSpec((tm, k), lambda i: (i, 0)),
                  _resident((k, n)), row, _resident((1, n)), _resident((1, n))],
        out_specs=row,
        out_shape=jax.ShapeDtypeStruct((m, n), F32),
        compiler_params=_params("parallel"),
        name="matmul_ln",
    )(a, w, res, g, b)


def _add_ple(h2, p_ref, wpg_ref, wpp_ref):
    gate = jnp.dot(h2.astype(BF16), wpg_ref[...], preferred_element_type=F32)
    proj = jnp.dot(p_ref[...].astype(BF16), wpp_ref[...], preferred_element_type=F32)
    return h2 + _sigmoid(gate) * proj


def _swiglu_ln_kernel(h_ref, wg_ref, wu_ref, wd_ref, g_ref, b_ref, p_ref, wpg_ref, wpp_ref,
                      o_ref, h_bf, acc_ref):
    f = pl.program_id(1)

    @pl.when(f == 0)
    def _():
        h_bf[...] = h_ref[...].astype(BF16)
        acc_ref[...] = jnp.zeros_like(acc_ref)

    a = h_bf[...]
    gate = jnp.dot(a, wg_ref[...], preferred_element_type=F32)
    up = jnp.dot(a, wu_ref[...], preferred_element_type=F32)
    mid = (_silu(gate) * up).astype(BF16)
    acc_ref[...] += jnp.dot(mid, wd_ref[...], preferred_element_type=F32)

    @pl.when(f == pl.num_programs(1) - 1)
    def _():
        x = DEEPNORM_ALPHA * h_ref[...] + acc_ref[...]
        h2 = _layer_norm(x, g_ref[...], b_ref[...])
        o_ref[...] = _add_ple(h2, p_ref, wpg_ref, wpp_ref)


def _swiglu_ln_ple(h, wg, wu, wd, g, b, p, wpg, wpp, tm=512, tf=512):
    m = h.shape[0]
    ff = wg.shape[1]
    return pl.pallas_call(
        _swiglu_ln_kernel,
        grid=(m // tm, ff // tf),
        in_specs=[pl.BlockSpec((tm, D_MODEL), lambda i, f: (i, 0)),
                  pl.BlockSpec((D_MODEL, tf), lambda i, f: (0, f)),
                  pl.BlockSpec((D_MODEL, tf), lambda i, f: (0, f)),
                  pl.BlockSpec((tf, D_MODEL), lambda i, f: (f, 0)),
                  _resident((1, D_MODEL)), _resident((1, D_MODEL)),
                  pl.BlockSpec((tm, PLE_DIM), lambda i, f: (i, 0)),
                  _resident((D_MODEL, D_MODEL)), _resident((PLE_DIM, D_MODEL))],
        out_specs=pl.BlockSpec((tm, D_MODEL), lambda i, f: (i, 0)),
        out_shape=jax.ShapeDtypeStruct((m, D_MODEL), F32),
        scratch_shapes=[pltpu.VMEM((tm, D_MODEL), BF16), pltpu.VMEM((tm, D_MODEL), F32)],
        compiler_params=_params("parallel", "arbitrary"),
        name="swiglu_ln_ple",
    )(h, wg, wu, wd, g, b, p, wpg, wpp)


def _latent_kernel(h_ref, w_ref, gkv_ref, gq_ref, tab_ref, ckv_ref, cq_ref, kr_ref):
    r = jnp.dot(h_ref[...].astype(BF16), w_ref[...], preferred_element_type=F32)
    ckv_ref[...] = _rms_norm(r[:, :KV_RANK], gkv_ref[...]).astype(BF16)
    cq_ref[...] = _rms_norm(r[:, KV_RANK:KV_RANK + Q_RANK], gq_ref[...]).astype(BF16)
    t = r[:, KV_RANK + Q_RANK:] * tab_ref[...]
    rot = t + pltpu.roll(t, ROPE_DIM, axis=1)
    lane = lax.broadcasted_iota(jnp.int32, rot.shape, 1)
    kr_ref[...] = jnp.where(lane < ROPE_DIM, rot, 0.0).astype(BF16)


def _latent(h, w_cat, g_kv, g_q, table, tm=512):
    n = w_cat.shape[1]
    return pl.pallas_call(
        _latent_kernel,
        grid=(TOKENS // tm,),
        in_specs=[pl.BlockSpec((tm, D_MODEL), lambda i: (i, 0)),
                  pl.BlockSpec((D_MODEL, n), lambda i: (0, 0)),
                  pl.BlockSpec((1, KV_RANK), lambda i: (0, 0)),
                  pl.BlockSpec((1, Q_RANK), lambda i: (0, 0)),
                  pl.BlockSpec((tm, 128), lambda i: (i, 0))],
        out_specs=[pl.BlockSpec((tm, KV_RANK), lambda i: (i, 0)),
                   pl.BlockSpec((tm, Q_RANK), lambda i: (i, 0)),
                   pl.BlockSpec((tm, 128), lambda i: (i, 0))],
        out_shape=[jax.ShapeDtypeStruct((TOKENS, KV_RANK), BF16),
                   jax.ShapeDtypeStruct((TOKENS, Q_RANK), BF16),
                   jax.ShapeDtypeStruct((TOKENS, 128), BF16)],
        compiler_params=_params("parallel"),
        name="latent",
    )(h, w_cat, g_kv, g_q, table)


def _kv_up_kernel(c_ref, wk_ref, wv_ref, kr_ref, k_ref, v_ref):
    c = c_ref[...]
    kn = jnp.dot(c, wk_ref[...], preferred_element_type=F32).astype(BF16)
    v_ref[...] = jnp.dot(c, wv_ref[...], preferred_element_type=F32).astype(BF16)
    kr = kr_ref[...]
    for h in range(MLA_HEADS):
        k_ref[:, h * HEAD_PAD:h * HEAD_PAD + NOPE_DIM] = kn[:, h * NOPE_DIM:(h + 1) * NOPE_DIM]
        k_ref[:, h * HEAD_PAD + NOPE_DIM:(h + 1) * HEAD_PAD] = kr


def _kv_up(c_kv, w_uk, w_uv, k_rope, tm=512):
    return pl.pallas_call(
        _kv_up_kernel,
        grid=(TOKENS // tm,),
        in_specs=[pl.BlockSpec((tm, KV_RANK), lambda i: (i, 0)),
                  pl.BlockSpec((KV_RANK, MLA_HEADS * NOPE_DIM), lambda i: (0, 0)),
                  pl.BlockSpec((KV_RANK, MLA_HEADS * V_DIM), lambda i: (0, 0)),
                  pl.BlockSpec((tm, 128), lambda i: (i, 0))],
        out_specs=[pl.BlockSpec((tm, MLA_HEADS * HEAD_PAD), lambda i: (i, 0)),
                   pl.BlockSpec((tm, MLA_HEADS * V_DIM), lambda i: (i, 0))],
        out_shape=[jax.ShapeDtypeStruct((TOKENS, MLA_HEADS * HEAD_PAD), BF16),
                   jax.ShapeDtypeStruct((TOKENS, MLA_HEADS * V_DIM), BF16)],
        compiler_params=_params("parallel"),
        name="kv_up",
    )(c_kv, w_uk, w_uv, k_rope)


def _q_up_kernel(c_ref, w_ref, tab_ref, q_ref):
    scale = (NOPE_DIM + ROPE_DIM) ** -0.5
    r = jnp.dot(c_ref[...], w_ref[...], preferred_element_type=F32)
    tab = tab_ref[...]
    for h in range(MLA_HEADS):
        base = h * HEAD_PAD
        q_ref[:, base:base + NOPE_DIM] = (r[:, base:base + NOPE_DIM] * scale).astype(BF16)
        t = r[:, base + NOPE_DIM:base + HEAD_PAD] * tab
        rot = (t + pltpu.roll(t, ROPE_DIM, axis=1)) * scale
        q_ref[:, base + NOPE_DIM:base + HEAD_PAD] = rot.astype(BF16)


def _q_up(c_q, w_q, table, tm=512):
    return pl.pallas_call(
        _q_up_kernel,
        grid=(TOKENS // tm,),
        in_specs=[pl.BlockSpec((tm, Q_RANK), lambda i: (i, 0)),
                  pl.BlockSpec((Q_RANK, MLA_HEADS * HEAD_PAD), lambda i: (0, 0)),
                  pl.BlockSpec((tm, 128), lambda i: (i, 0))],
        out_specs=pl.BlockSpec((tm, MLA_HEADS * HEAD_PAD), lambda i: (i, 0)),
        out_shape=jax.ShapeDtypeStruct((TOKENS, MLA_HEADS * HEAD_PAD), BF16),
        compiler_params=_params("parallel"),
        name="q_up",
    )(c_q, w_q, table)


ATT_TQ = 512
ATT_TK = 512


def _attn_kernel(q_ref, k_ref, v_ref, o_ref):
    row = lax.broadcasted_iota(jnp.int32, (ATT_TQ, ATT_TK), 0)
    col = lax.broadcasted_iota(jnp.int32, (ATT_TQ, ATT_TK), 1)
    causal = row >= col
    for qi in range(SEQ // ATT_TQ):
        q = q_ref[qi * ATT_TQ:(qi + 1) * ATT_TQ, :]
        m = l = acc = None
        for kj in range(qi + 1):
            k = k_ref[kj * ATT_TK:(kj + 1) * ATT_TK, :]
            v = v_ref[kj * ATT_TK:(kj + 1) * ATT_TK, :]
            s = lax.dot_general(q, k, (((1,), (1,)), ((), ())), preferred_element_type=F32)
            if kj == qi:
                s = jnp.where(causal, s, -jnp.inf)
            s_max = jnp.max(s, axis=-1, keepdims=True)
            if kj == 0:
                m = s_max
                p = jnp.exp(s - m)
                l = jnp.sum(p, axis=-1, keepdims=True)
                acc = jnp.dot(p.astype(BF16), v, preferred_element_type=F32)
            else:
                m_new = jnp.maximum(m, s_max)
                alpha = jnp.exp(m - m_new)
                p = jnp.exp(s - m_new)
                l = alpha * l + jnp.sum(p, axis=-1, keepdims=True)
                acc = alpha * acc + jnp.dot(p.astype(BF16), v, preferred_element_type=F32)
                m = m_new
        o_ref[qi * ATT_TQ:(qi + 1) * ATT_TQ, :] = (acc / l).astype(o_ref.dtype)


def _attention(q_cat, k_cat, v):
    return pl.pallas_call(
        _attn_kernel,
        grid=(BATCH, MLA_HEADS),
        in_specs=[pl.BlockSpec((SEQ, HEAD_PAD), lambda b, h: (b, h)),
                  pl.BlockSpec((SEQ, HEAD_PAD), lambda b, h: (b, h)),
                  pl.BlockSpec((SEQ, V_DIM), lambda b, h: (b, h))],
        out_specs=pl.BlockSpec((SEQ, V_DIM), lambda b, h: (b, h)),
        out_shape=jax.ShapeDtypeStruct((TOKENS, MLA_HEADS * V_DIM), BF16),
        compiler_params=_params("parallel", "parallel"),
        name="attention",
    )(q_cat, k_cat, v)


def _router_kernel(h_ref, w_hi_ref, w_lo_ref, b_ref, tri_ref, o_ref, cnt_ref, carry_ref):
    @pl.when(pl.program_id(0) == 0)
    def _():
        carry_ref[...] = jnp.zeros_like(carry_ref)

    logits = _dot_split(h_ref[...], w_hi_ref, w_lo_ref) + b_ref[...]
    lane = lax.broadcasted_iota(jnp.int32, logits.shape, 1).astype(F32)
    logits = jnp.where(lane < N_EXPERTS, logits, -jnp.inf)
    m1 = jnp.max(logits, axis=-1, keepdims=True)
    i1 = jnp.min(jnp.where(logits == m1, lane, 128.0), axis=-1, keepdims=True)
    rest = jnp.where(lane == i1, -jnp.inf, logits)
    m2 = jnp.max(rest, axis=-1, keepdims=True)
    i2 = jnp.min(jnp.where(rest == m2, lane, 128.0), axis=-1, keepdims=True)
    e2 = jnp.exp(m2 - m1)
    w1 = 1.0 / (1.0 + e2)
    w2 = e2 / (1.0 + e2)
    first = lane == i1
    second = lane == i2
    chosen = jnp.where(jnp.logical_or(first, second), 1.0, 0.0)
    before = carry_ref[0:1, :] + jnp.dot(tri_ref[...], chosen.astype(BF16),
                                         preferred_element_type=F32)
    r1 = jnp.sum(jnp.where(first, before, 0.0), axis=-1, keepdims=True)
    r2 = jnp.sum(jnp.where(second, before, 0.0), axis=-1, keepdims=True)
    total = carry_ref[0:1, :] + jnp.sum(chosen, axis=0, keepdims=True)
    carry_ref[...] = jnp.broadcast_to(total, carry_ref.shape)
    cnt_ref[...] = jnp.broadcast_to(total, cnt_ref.shape)
    out = jnp.where(lane == 0.0, i1, jnp.where(lane == 1.0, i2, 0.0))
    out = jnp.where(lane == 2.0, w1, jnp.where(lane == 3.0, w2, out))
    out = jnp.where(lane == 4.0, r1, jnp.where(lane == 5.0, r2, out))
    o_ref[...] = out


def _router(h, w_router, b_router, tm=512):
    w_hi, w_lo = _split_weight(jnp.pad(w_router, ((0, 0), (0, 128 - N_EXPERTS))))
    b_pad = jnp.pad(b_router, ((0, 0), (0, 128 - N_EXPERTS)))
    tri = jnp.tril(jnp.ones((tm, tm), BF16), k=-1)
    return pl.pallas_call(
        _router_kernel,
        grid=(TOKENS // tm,),
        in_specs=[pl.BlockSpec((tm, D_MODEL), lambda i: (i, 0)),
                  pl.BlockSpec((D_MODEL, 128), lambda i: (0, 0)),
                  pl.BlockSpec((D_MODEL, 128), lambda i: (0, 0)),
                  pl.BlockSpec((1, 128), lambda i: (0, 0)),
                  pl.BlockSpec((tm, tm), lambda i: (0, 0))],
        out_specs=[pl.BlockSpec((tm, 128), lambda i: (i, 0)),
                   pl.BlockSpec((8, 128), lambda i: (0, 0))],
        out_shape=[jax.ShapeDtypeStruct((TOKENS, 128), F32),
                   jax.ShapeDtypeStruct((8, 128), F32)],
        scratch_shapes=[pltpu.VMEM((8, 128), F32)],
        compiler_params=_params("arbitrary"),
        name="router",
    )(h, w_hi, w_lo, b_pad, tri)


MOE_TM = 1024
MOE_TF = 256
MOE_ROWS = 2 * TOKENS + N_EXPERTS * MOE_TM
MOE_TILES = MOE_ROWS // MOE_TM


MOE_NF = D_FF_EXPERT // MOE_TF
MOE_ISSUE = 96
MOE_BUF_ROWS = MOE_NF * MOE_ISSUE
assert MOE_BUF_ROWS >= MOE_TM


def _moe_kernel(te_ref, nt_ref, src_ref, h_hbm, wg_ref, wu_ref, wd_ref, o_ref,
                xbuf, x_bf, acc_ref, sem):
    i = pl.program_id(0)
    f = pl.program_id(1)
    last = pl.num_programs(1) - 1
    n_valid = nt_ref[0]
    valid = i < n_valid
    slot = lax.rem(i, 2)

    def row_copy(tile, r, s):
        tok = src_ref[jnp.minimum(tile, MOE_TILES - 1) * MOE_TM + jnp.minimum(r, MOE_TM - 1)]
        return pltpu.make_async_copy(h_hbm.at[pl.ds(tok, 1), :], xbuf.at[s, pl.ds(r, 1), :],
                                     sem.at[s])

    def wait_rows(s):
        def body(r, c):
            row_copy(0, 0, s).wait()
            return c
        lax.fori_loop(0, MOE_BUF_ROWS, body, 0)

    @pl.when(jnp.logical_and(i == 0, f == 0))
    def _():
        def body(r, c):
            row_copy(0, r, 0).start()
            return c
        lax.fori_loop(0, MOE_BUF_ROWS, body, 0, unroll=8)

    @pl.when(jnp.logical_and(f == 0, i <= n_valid))
    def _():
        wait_rows(slot)

    @pl.when(valid)
    def _():
        @pl.when(f == 0)
        def _():
            x_bf[...] = xbuf[slot, 0:MOE_TM, :].astype(BF16)
            acc_ref[...] = jnp.zeros_like(acc_ref)

        for j in range(MOE_ISSUE):
            row_copy(i + 1, f * MOE_ISSUE + j, 1 - slot).start()

        a = x_bf[...]
        gate = jnp.dot(a, wg_ref[0].astype(BF16), preferred_element_type=F32)
        up = jnp.dot(a, wu_ref[0].astype(BF16), preferred_element_type=F32)
        mid = (_silu(gate) * up).astype(BF16)
        acc_ref[...] += jnp.dot(mid, wd_ref[0].astype(BF16), preferred_element_type=F32)

        @pl.when(f == last)
        def _():
            o_ref[...] = acc_ref[...].astype(o_ref.dtype)

        @pl.when(jnp.logical_and(f == last, i == MOE_TILES - 1))
        def _():
            wait_rows(1 - slot)

    @pl.when(jnp.logical_and(jnp.logical_not(valid), f == last))
    def _():
        o_ref[...] = jnp.zeros_like(o_ref)


def _moe_experts(tile_expert, n_tiles, src, h, wg, wu, wd):
    def f_of(i, f, nt):
        return jnp.where(i < nt[0], f, MOE_NF - 1)

    grid_spec = pltpu.PrefetchScalarGridSpec(
        num_scalar_prefetch=3,
        grid=(MOE_TILES, MOE_NF),
        in_specs=[
            pl.BlockSpec(memory_space=pl.ANY),
            pl.BlockSpec((1, D_MODEL, MOE_TF),
                         lambda i, f, te, nt, src: (te[i], 0, f_of(i, f, nt))),
            pl.BlockSpec((1, D_MODEL, MOE_TF),
                         lambda i, f, te, nt, src: (te[i], 0, f_of(i, f, nt))),
            pl.BlockSpec((1, MOE_TF, D_MODEL),
                         lambda i, f, te, nt, src: (te[i], f_of(i, f, nt), 0)),
        ],
        out_specs=pl.BlockSpec((MOE_TM, D_MODEL), lambda i, f, te, nt, src: (i, 0)),
        scratch_shapes=[pltpu.VMEM((2, MOE_BUF_ROWS, D_MODEL), F32),
                        pltpu.VMEM((MOE_TM, D_MODEL), BF16),
                        pltpu.VMEM((MOE_TM, D_MODEL), F32),
                        pltpu.SemaphoreType.DMA((2,))],
    )
    return pl.pallas_call(
        _moe_kernel,
        grid_spec=grid_spec,
        out_shape=jax.ShapeDtypeStruct((MOE_ROWS, D_MODEL), BF16),
        compiler_params=_params("arbitrary", "arbitrary"),
        name="moe_experts",
    )(tile_expert, n_tiles, src, h, wg, wu, wd)


def _combine_ln_kernel(h_ref, y0_ref, y1_ref, w0_ref, w1_ref, g_ref, b_ref, p_ref, wpg_ref,
                       wpp_ref, o_ref):
    ff = w0_ref[...] * y0_ref[...].astype(F32) + w1_ref[...] * y1_ref[...].astype(F32)
    h2 = _layer_norm(DEEPNORM_ALPHA * h_ref[...] + ff, g_ref[...], b_ref[...])
    o_ref[...] = _add_ple(h2, p_ref, wpg_ref, wpp_ref)


def _combine_ln_ple(h, y0, y1, w0, w1, g, b, p, wpg, wpp, tm=512):
    row = pl.BlockSpec((tm, D_MODEL), lambda i: (i, 0))
    col = pl.BlockSpec((tm, 1), lambda i: (i, 0))
    return pl.pallas_call(
        _combine_ln_kernel,
        grid=(TOKENS // tm,),
        in_specs=[row, row, row, col, col, _resident((1, D_MODEL)), _resident((1, D_MODEL)),
                  pl.BlockSpec((tm, PLE_DIM), lambda i: (i, 0)),
                  _resident((D_MODEL, D_MODEL)), _resident((PLE_DIM, D_MODEL))],
        out_specs=row,
        out_shape=jax.ShapeDtypeStruct((TOKENS, D_MODEL), F32),
        compiler_params=_params("parallel"),
        name="combine_ln_ple",
    )(h, y0, y1, w0, w1, g, b, p, wpg, wpp)


def _swap_halves(w):
    half = w.shape[-1] // 2
    return jnp.concatenate([w[..., half:], w[..., :half]], axis=-1)


def kernel(x, p, positions, ssm_w_in, ssm_conv_w, ssm_conv_b, ssm_dt_bias, ssm_a_log, ssm_d, ssm_norm_g, ssm_w_out, kv_w_down, kv_norm_g, kv_w_rope, kv_w_uk, kv_w_uv, mla_w_dq, mla_q_norm_g, mla_w_uq, mla_w_o, ffn_w_gate, ffn_w_up, ffn_w_down, moe_w_router, moe_b_router, moe_w_gate, moe_w_up, moe_w_down, ln1_g, ln1_b, ln2_g, ln2_b, ple_w_proj, ple_w_gate):
    h0 = x.reshape(TOKENS, D_MODEL)
    p = p.reshape(DEPTH, TOKENS, PLE_DIM)
    row = lambda v: v.reshape(1, -1)

    zx_cols = D_INNER + CONV_DIM
    w_dt = ssm_w_in[0][:, zx_cols:]
    zx = _matmul(h0, ssm_w_in[0].astype(BF16), zx_cols, BF16, tm=1024, tn=1024)
    parts, acum = _dt_prep(h0, w_dt, row(ssm_dt_bias[0]), row(ssm_a_log[0]))
    xbc = _conv_silu(zx, ssm_conv_w[0], row(ssm_conv_b[0]))
    parts = parts.reshape(N_WIDE * N_SPLIT, BATCH, SEQ, SSM_GROUPS, HEADS_PER_GROUP)
    parts = parts.transpose(1, 3, 2, 0, 4).reshape(BATCH, SSM_GROUPS, SEQ, WIDE_ROWS)
    acumt = acum.reshape(BATCH, N_CHUNKS, SSM_CHUNK, SSM_GROUPS, HEADS_PER_GROUP)
    acumt = acumt.transpose(0, 3, 1, 4, 2)
    d_full = row(jnp.repeat(ssm_d[0], SSM_HEAD_DIM))
    y = _ssd(xbc, zx, parts, acumt, d_full, row(ssm_norm_g[0]))
    h = _matmul_ln(y, ssm_w_out[0].astype(BF16), h0, row(ln1_g[0]), row(ln1_b[0]))

    h = _swiglu_ln_ple(h, ffn_w_gate[0].astype(BF16), ffn_w_up[0].astype(BF16),
                       ffn_w_down[0].astype(BF16), row(ln2_g[0]), row(ln2_b[0]),
                       p[0], ple_w_gate[0].astype(BF16), ple_w_proj[0].astype(BF16))

    inv_freq = ROPE_THETA ** (-jnp.arange(0, ROPE_DIM, 2, dtype=F32) / ROPE_DIM)
    table = _rope_table(positions.reshape(TOKENS, 1), row(jnp.tile(inv_freq, 4)))
    w_lat = jnp.concatenate([kv_w_down, mla_w_dq[0], kv_w_rope, _swap_halves(kv_w_rope)],
                            axis=1).astype(BF16)
    c_kv, c_q, k_rope = _latent(h, w_lat, row(kv_norm_g), row(mla_q_norm_g[0]), table)
    k_cat, v = _kv_up(c_kv, kv_w_uk.astype(BF16), kv_w_uv.astype(BF16), k_rope)
    w_uq = mla_w_uq[0].reshape(Q_RANK, MLA_HEADS, NOPE_DIM + ROPE_DIM)
    w_q = jnp.concatenate([w_uq, _swap_halves(w_uq[..., NOPE_DIM:])], axis=-1)
    w_q = w_q.reshape(Q_RANK, MLA_HEADS * HEAD_PAD).astype(BF16)
    q_cat = _q_up(c_q, w_q, table)
    o = _attention(q_cat, k_cat, v)
    h = _matmul_ln(o, mla_w_o[0].astype(BF16), h, row(ln1_g[1]), row(ln1_b[1]))

    route, counts = _router(h, moe_w_router[0], row(moe_b_router[0]))
    counts = counts[0, :N_EXPERTS].astype(jnp.int32)
    tiles_per = (counts + MOE_TM - 1) // MOE_TM
    tile_end = jnp.cumsum(tiles_per)
    start = (tile_end - tiles_per) * MOE_TM
    expert = route[:, 0:2].astype(jnp.int32)
    rank = route[:, 4:6].astype(jnp.int32)
    start_of = jnp.sum(jnp.where(expert[..., None] == jnp.arange(N_EXPERTS, dtype=jnp.int32),
                                 start, 0), axis=-1)
    pos = start_of + rank
    n_tiles = tile_end[-1:].astype(jnp.int32)
    tile_ids = jnp.minimum(jnp.arange(MOE_TILES, dtype=jnp.int32), n_tiles[0] - 1)
    tile_expert = jnp.searchsorted(tile_end, tile_ids, side="right").astype(jnp.int32)
    token = jnp.broadcast_to(jnp.arange(TOKENS, dtype=jnp.int32)[:, None], (TOKENS, 2))
    src = jnp.zeros((MOE_ROWS,), jnp.int32).at[pos.reshape(-1)].set(
        token.reshape(-1), unique_indices=True)
    y_sorted = _moe_experts(tile_expert, n_tiles, src, h, moe_w_gate[0], moe_w_up[0],
                            moe_w_down[0])
    y0 = y_sorted.at[pos[:, 0]].get(mode="promise_in_bounds")
    y1 = y_sorted.at[pos[:, 1]].get(mode="promise_in_bounds")
    h = _combine_ln_ple(h, y0, y1, route[:, 2:3], route[:, 3:4], row(ln2_g[1]), row(ln2_b[1]),
                        p[1], ple_w_gate[1].astype(BF16), ple_w_proj[1].astype(BF16))
    return h.reshape(BATCH, SEQ, D_MODEL)
```

```python
import functools

import jax
import jax.numpy as jnp
from jax import lax
from jax.experimental import pallas as pl
from jax.experimental.pallas import tpu as pltpu

F32 = jnp.float32
BF16 = jnp.bfloat16
U32 = jnp.uint32
HIGHEST = lax.Precision.HIGHEST

D_MODEL = 2048
BATCH = 16
SEQ = 2048
TOKENS = BATCH * SEQ
DEPTH = 2

D_INNER = 4096
SSM_HEAD_DIM = 64
SSM_HEADS = 64
SSM_GROUPS = 8
HEADS_PER_GROUP = 8
SSM_STATE = 128
GROUP_WIDTH = HEADS_PER_GROUP * SSM_HEAD_DIM
CONV_K = 4
CONV_DIM = D_INNER + 2 * SSM_GROUPS * SSM_STATE
SSM_CHUNK = 128
N_CHUNKS = SEQ // SSM_CHUNK

MLA_HEADS = 16
Q_RANK = 512
KV_RANK = 512
NOPE_DIM = 128
ROPE_DIM = 64
V_DIM = 128
ROPE_THETA = 10000.0
HEAD_PAD = 256

D_FF = 5632
N_EXPERTS = 8
D_FF_EXPERT = 2816

PLE_DIM = 256
DEEPNORM_ALPHA = (2.0 * DEPTH) ** 0.25
LN_EPS = 1e-5
RMS_EPS = 1e-6

VMEM_LIMIT = 56 * 1024 * 1024


def _params(*sem):
    return pltpu.CompilerParams(dimension_semantics=sem, vmem_limit_bytes=VMEM_LIMIT)


def _layer_norm(x, g, b):
    mu = jnp.mean(x, axis=-1, keepdims=True)
    xc = x - mu
    var = jnp.mean(xc * xc, axis=-1, keepdims=True)
    return xc * lax.rsqrt(var + LN_EPS) * g + b


def _rms_norm(x, g):
    return x * lax.rsqrt(jnp.mean(x * x, axis=-1, keepdims=True) + RMS_EPS) * g


def _silu(x):
    return x * (1.0 / (1.0 + jnp.exp(-x)))


def _sigmoid(x):
    return 1.0 / (1.0 + jnp.exp(-x))


def _rope_table_kernel(pos_ref, freq_ref, o_ref):
    ang = pos_ref[...].astype(F32) * freq_ref[...]
    lane = lax.broadcasted_iota(jnp.int32, ang.shape, 1)
    c = jnp.cos(ang)
    s = jnp.sin(ang)
    o_ref[...] = jnp.where(lane < 64, c, jnp.where(lane < 96, -s, s))


def _rope_table(pos_col, freq_row, tm=1024):
    return pl.pallas_call(
        _rope_table_kernel,
        grid=(TOKENS // tm,),
        in_specs=[pl.BlockSpec((tm, 1), lambda i: (i, 0)),
                  pl.BlockSpec((1, 128), lambda i: (0, 0))],
        out_specs=pl.BlockSpec((tm, 128), lambda i: (i, 0)),
        out_shape=jax.ShapeDtypeStruct((TOKENS, 128), F32),
        compiler_params=_params("parallel"),
        name="rope_table",
    )(pos_col, freq_row)


def _matmul_kernel(a_ref, w_ref, o_ref, a_bf):
    @pl.when(pl.program_id(1) == 0)
    def _():
        a_bf[...] = a_ref[...].astype(BF16)

    o_ref[...] = jnp.dot(a_bf[...], w_ref[...], preferred_element_type=F32).astype(o_ref.dtype)


def _matmul(a, w, n, out_dtype, tm, tn):
    m, k = a.shape
    return pl.pallas_call(
        _matmul_kernel,
        grid=(m // tm, n // tn),
        in_specs=[pl.BlockSpec((tm, k), lambda i, j: (i, 0)),
                  pl.BlockSpec((k, tn), lambda i, j: (0, j))],
        out_specs=pl.BlockSpec((tm, tn), lambda i, j: (i, j)),
        out_shape=jax.ShapeDtypeStruct((m, n), out_dtype),
        scratch_shapes=[pltpu.VMEM((tm, k), BF16)],
        compiler_params=_params("parallel", "arbitrary"),
        name="matmul",
    )(a, w)


N_SPLIT = 3
N_WIDE = 4


def _split_bf16(v):
    parts = []
    for _ in range(N_SPLIT):
        part = v.astype(BF16)
        parts.append(part)
        v = v - part.astype(F32)
    return parts


def _dot_split(x, w_hi_ref, w_lo_ref):
    x_hi = x.astype(BF16)
    x_lo = (x - x_hi.astype(F32)).astype(BF16)
    w_hi = w_hi_ref[...]
    return (jnp.dot(x_hi, w_hi, preferred_element_type=F32)
            + jnp.dot(x_lo, w_hi, preferred_element_type=F32)
            + jnp.dot(x_hi, w_lo_ref[...], preferred_element_type=F32))


def _split_weight(w):
    w_hi = w.astype(BF16)
    return w_hi, (w - w_hi.astype(F32)).astype(BF16)


def _dt_kernel(x_ref, w_hi_ref, w_lo_ref, bias_ref, alog_ref, parts_ref, acum_ref, *, tm):
    raw = _dot_split(x_ref[...], w_hi_ref, w_lo_ref) + bias_ref[...]
    dt = jnp.maximum(raw, 0.0) + jnp.log1p(jnp.exp(-jnp.abs(raw)))
    a = -jnp.exp(alog_ref[...]) * dt
    row = lax.broadcasted_iota(jnp.int32, (SSM_CHUNK, SSM_CHUNK), 0)
    col = lax.broadcasted_iota(jnp.int32, (SSM_CHUNK, SSM_CHUNK), 1)
    tril = jnp.where(row >= col, 1.0, 0.0).astype(F32)
    for c in range(tm // SSM_CHUNK):
        sl = slice(c * SSM_CHUNK, (c + 1) * SSM_CHUNK)
        ac = jnp.dot(tril, a[sl, :], preferred_element_type=F32, precision=HIGHEST)
        acum_ref[sl, :] = ac
        ac_end = ac[SSM_CHUNK - 1:SSM_CHUNK, :]
        for qty, val in enumerate((dt[sl, :], jnp.exp(ac), jnp.exp(ac_end - ac), ac)):
            for part, piece in enumerate(_split_bf16(val)):
                parts_ref[qty * N_SPLIT + part, sl, :] = piece


def _dt_prep(x, w_dt, dt_bias, a_log, tm=512):
    n_parts = N_WIDE * N_SPLIT
    w_hi, w_lo = _split_weight(w_dt)
    return pl.pallas_call(
        functools.partial(_dt_kernel, tm=tm),
        grid=(TOKENS // tm,),
        in_specs=[pl.BlockSpec((tm, D_MODEL), lambda i: (i, 0)),
                  pl.BlockSpec((D_MODEL, SSM_HEADS), lambda i: (0, 0)),
                  pl.BlockSpec((D_MODEL, SSM_HEADS), lambda i: (0, 0)),
                  pl.BlockSpec((1, SSM_HEADS), lambda i: (0, 0)),
                  pl.BlockSpec((1, SSM_HEADS), lambda i: (0, 0))],
        out_specs=[pl.BlockSpec((n_parts, tm, SSM_HEADS), lambda i: (0, i, 0)),
                   pl.BlockSpec((tm, SSM_HEADS), lambda i: (i, 0))],
        out_shape=[jax.ShapeDtypeStruct((n_parts, TOKENS, SSM_HEADS), BF16),
                   jax.ShapeDtypeStruct((TOKENS, SSM_HEADS), F32)],
        compiler_params=_params("parallel"),
        name="dt_prep",
    )(x, w_hi, w_lo, dt_bias, a_log)


CONV_ROWS = 256
CONV_HALO = 16


def _conv_kernel(u_ref, w_ref, b_ref, o_ref):
    w = w_ref[...]
    b = b_ref[...]
    tc = u_ref.shape[1]
    for i in range(SEQ // CONV_ROWS):
        r = i * CONV_ROWS
        cur = u_ref[r:r + CONV_ROWS, :].astype(F32)
        if i == 0:
            halo = jnp.zeros((CONV_HALO, tc), F32)
        else:
            halo = u_ref[r - CONV_HALO:r, :].astype(F32)
        win = jnp.concatenate([halo, cur], axis=0)
        acc = b + w[CONV_K - 1:CONV_K, :] * cur
        for k in range(CONV_K - 1):
            shifted = pltpu.roll(win, CONV_K - 1 - k, axis=0)[CONV_HALO:, :]
            acc = acc + w[k:k + 1, :] * shifted
        o_ref[r:r + CONV_ROWS, :] = _silu(acc).astype(o_ref.dtype)


def _conv_silu(zx, conv_w, conv_b, tc=512):
    col0 = D_INNER // tc
    return pl.pallas_call(
        _conv_kernel,
        grid=(BATCH, CONV_DIM // tc),
        in_specs=[pl.BlockSpec((SEQ, tc), lambda b, j: (b, col0 + j)),
                  pl.BlockSpec((CONV_K, tc), lambda b, j: (0, j)),
                  pl.BlockSpec((1, tc), lambda b, j: (0, j))],
        out_specs=pl.BlockSpec((SEQ, tc), lambda b, j: (b, j)),
        out_shape=jax.ShapeDtypeStruct((TOKENS, CONV_DIM), BF16),
        compiler_params=_params("parallel", "parallel"),
        name="conv_silu",
    )(zx, conv_w, conv_b)


WIDE_ROWS = N_WIDE * N_SPLIT * HEADS_PER_GROUP
WIDE_ACUM = 3 * GROUP_WIDTH
WIDE_COLS = WIDE_ACUM + HEADS_PER_GROUP * SSM_CHUNK


def _widen_matrix():
    r = jnp.arange(WIDE_ROWS)[:, None]
    c = jnp.arange(WIDE_COLS)[None, :]
    qty = r // (N_SPLIT * HEADS_PER_GROUP)
    head = r % HEADS_PER_GROUP
    chan = (c // GROUP_WIDTH == qty) & ((c % GROUP_WIDTH) // SSM_HEAD_DIM == head) & (c < WIDE_ACUM)
    copy = (qty == N_WIDE - 1) & (c >= WIDE_ACUM) & ((c - WIDE_ACUM) // SSM_CHUNK == head)
    return (chan | copy).astype(BF16)


def _ssd_kernel(x_ref, b_ref, c_ref, z_ref, parts_ref, acumt_ref, widen_ref, d_ref, g_ref,
                o_ref, state_ref):
    q = SSM_CHUNK
    state_ref[...] = jnp.zeros_like(state_ref)
    trow = lax.broadcasted_iota(jnp.int32, (q, q), 0)
    tcol = lax.broadcasted_iota(jnp.int32, (q, q), 1)
    causal = trow >= tcol
    first_head = lax.broadcasted_iota(jnp.int32, (q, 2 * SSM_HEAD_DIM), 1) < SSM_HEAD_DIM

    def chunk(c, carry):
        rows = pl.ds(pl.multiple_of(c * q, q), q)
        xf = x_ref[rows, :].astype(F32)
        bm = b_ref[rows, :]
        cm = c_ref[rows, :]
        act = acumt_ref[0, 0, c]
        wide = jnp.dot(parts_ref[0, 0, rows, :], widen_ref[...], preferred_element_type=F32)
        xdt = xf * wide[:, :GROUP_WIDTH]
        decay_out = wide[:, GROUP_WIDTH:2 * GROUP_WIDTH]
        to_end = wide[:, 2 * GROUP_WIDTH:WIDE_ACUM]
        xdt_bf = xdt.astype(BF16)

        cb = lax.dot_general(cm, bm, (((1,), (1,)), ((), ())), preferred_element_type=F32)
        state = state_ref[...]
        y = jnp.dot(cm, state.astype(BF16), preferred_element_type=F32) * decay_out

        pairs = []
        for pr in range(HEADS_PER_GROUP // 2):
            xpair = xdt_bf[:, pr * 128:(pr + 1) * 128]
            ms = []
            for j in (2 * pr, 2 * pr + 1):
                col0 = WIDE_ACUM + j * q
                seg = jnp.where(causal, wide[:, col0:col0 + q] - act[j:j + 1, :], -jnp.inf)
                ms.append((cb * jnp.exp(seg)).astype(BF16))
            zero = jnp.zeros_like(xpair)
            rhs = jnp.concatenate([jnp.where(first_head, xpair, zero),
                                   jnp.where(first_head, zero, xpair)], axis=0)
            pairs.append(jnp.dot(jnp.concatenate(ms, axis=1), rhs, preferred_element_type=F32))
        y = y + jnp.concatenate(pairs, axis=1)

        upd = lax.dot_general(bm, (xdt * to_end).astype(BF16), (((0,), (0,)), ((), ())),
                              preferred_element_type=F32)
        state_ref[...] = state * decay_out[q - 1:q, :] + upd

        y = y + d_ref[...] * xf
        y = y * _silu(z_ref[rows, :].astype(F32))
        o_ref[rows, :] = _rms_norm(y, g_ref[...]).astype(o_ref.dtype)
        return carry

    lax.fori_loop(0, N_CHUNKS, chunk, 0, unroll=2)


def _ssd(xbc, zx, parts, acumt, d_full, norm_g):
    x_cols = D_INNER // SSM_STATE
    nb = SSM_GROUPS
    return pl.pallas_call(
        _ssd_kernel,
        grid=(BATCH, SSM_GROUPS),
        in_specs=[
            pl.BlockSpec((SEQ, GROUP_WIDTH), lambda b, g: (b, g)),
            pl.BlockSpec((SEQ, SSM_STATE), lambda b, g: (b, x_cols + g)),
            pl.BlockSpec((SEQ, SSM_STATE), lambda b, g: (b, x_cols + nb + g)),
            pl.BlockSpec((SEQ, GROUP_WIDTH), lambda b, g: (b, g)),
            pl.BlockSpec((1, 1, SEQ, WIDE_ROWS), lambda b, g: (b, g, 0, 0)),
            pl.BlockSpec((1, 1, N_CHUNKS, HEADS_PER_GROUP, SSM_CHUNK), lambda b, g: (b, g, 0, 0, 0)),
            pl.BlockSpec((WIDE_ROWS, WIDE_COLS), lambda b, g: (0, 0)),
            pl.BlockSpec((1, GROUP_WIDTH), lambda b, g: (0, g)),
            pl.BlockSpec((1, GROUP_WIDTH), lambda b, g: (0, g)),
        ],
        out_specs=pl.BlockSpec((SEQ, GROUP_WIDTH), lambda b, g: (b, g)),
        out_shape=jax.ShapeDtypeStruct((TOKENS, D_INNER), BF16),
        scratch_shapes=[pltpu.VMEM((SSM_STATE, GROUP_WIDTH), F32)],
        compiler_params=_params("parallel", "parallel"),
        name="ssd_scan",
    )(xbc, xbc, xbc, zx, parts, acumt, _widen_matrix(), d_full, norm_g)


HIGH_HALF = 0xFFFF0000


def _pack_bf16_halves(x):
    half = x.shape[1] // 2
    lo = pltpu.bitcast(x[:, :half].astype(BF16).astype(F32), U32)
    hi = pltpu.bitcast(x[:, half:].astype(BF16).astype(F32), U32)
    return lax.shift_right_logical(lo, U32(16)) | (hi & U32(HIGH_HALF))


def _unpack_bf16_halves(words):
    lo = pltpu.bitcast(lax.shift_left(words, U32(16)), F32)
    hi = pltpu.bitcast(words & U32(HIGH_HALF), F32)
    return lo.astype(BF16), hi.astype(BF16)


def _matmul_ln_kernel(a_ref, w_ref, res_ref, g_ref, b_ref, o_ref, *packed_ref):
    mix = jnp.dot(a_ref[...], w_ref[...], preferred_element_type=F32)
    out = _layer_norm(DEEPNORM_ALPHA * res_ref[...] + mix, g_ref[...], b_ref[...])
    o_ref[...] = out
    for ref in packed_ref:
        ref[...] = _pack_bf16_halves(out)


def _resident(shape):
    return pl.BlockSpec(shape, lambda *_: (0,) * len(shape), pipeline_mode=pl.Buffered(1))


def _matmul_ln(a, w, res, g, b, also_packed, tm=512):
    m, k = a.shape
    n = w.shape[1]
    row = pl.BlockSpec((tm, n), lambda i: (i, 0))
    out_specs = [row]
    out_shape = [jax.ShapeDtypeStruct((m, n), F32)]
    if also_packed:
        out_specs.append(pl.BlockSpec((tm, n // 2), lambda i: (i, 0)))
        out_shape.append(jax.ShapeDtypeStruct((m, n // 2), U32))
    return pl.pallas_call(
        _matmul_ln_kernel,
        grid=(m // tm,),
        in_specs=[pl.BlockSpec((tm, k), lambda i: (i, 0)),
                  _resident((k, n)), row, _resident((1, n)), _resident((1, n))],
        out_specs=out_specs,
        out_shape=out_shape,
        compiler_params=_params("parallel"),
        name="matmul_ln",
    )(a, w, res, g, b)


def _add_ple(h2, p_ref, wpg_ref, wpp_ref):
    gate = jnp.dot(h2.astype(BF16), wpg_ref[...], preferred_element_type=F32)
    proj = jnp.dot(p_ref[...].astype(BF16), wpp_ref[...], preferred_element_type=F32)
    return h2 + _sigmoid(gate) * proj


def _swiglu_ln_kernel(h_ref, wg_ref, wu_ref, wd_ref, g_ref, b_ref, p_ref, wpg_ref, wpp_ref,
                      o_ref, h_bf, acc_ref):
    f = pl.program_id(1)

    @pl.when(f == 0)
    def _():
        h_bf[...] = h_ref[...].astype(BF16)
        acc_ref[...] = jnp.zeros_like(acc_ref)

    a = h_bf[...]
    gate = jnp.dot(a, wg_ref[...], preferred_element_type=F32)
    up = jnp.dot(a, wu_ref[...], preferred_element_type=F32)
    mid = (_silu(gate) * up).astype(BF16)
    acc_ref[...] += jnp.dot(mid, wd_ref[...], preferred_element_type=F32)

    @pl.when(f == pl.num_programs(1) - 1)
    def _():
        x = DEEPNORM_ALPHA * h_ref[...] + acc_ref[...]
        h2 = _layer_norm(x, g_ref[...], b_ref[...])
        o_ref[...] = _add_ple(h2, p_ref, wpg_ref, wpp_ref)


def _swiglu_ln_ple(h, wg, wu, wd, g, b, p, wpg, wpp, tm=512, tf=512):
    m = h.shape[0]
    ff = wg.shape[1]
    return pl.pallas_call(
        _swiglu_ln_kernel,
        grid=(m // tm, ff // tf),
        in_specs=[pl.BlockSpec((tm, D_MODEL), lambda i, f: (i, 0)),
                  pl.BlockSpec((D_MODEL, tf), lambda i, f: (0, f)),
                  pl.BlockSpec((D_MODEL, tf), lambda i, f: (0, f)),
                  pl.BlockSpec((tf, D_MODEL), lambda i, f: (f, 0)),
                  _resident((1, D_MODEL)), _resident((1, D_MODEL)),
                  pl.BlockSpec((tm, PLE_DIM), lambda i, f: (i, 0)),
                  _resident((D_MODEL, D_MODEL)), _resident((PLE_DIM, D_MODEL))],
        out_specs=pl.BlockSpec((tm, D_MODEL), lambda i, f: (i, 0)),
        out_shape=jax.ShapeDtypeStruct((m, D_MODEL), F32),
        scratch_shapes=[pltpu.VMEM((tm, D_MODEL), BF16), pltpu.VMEM((tm, D_MODEL), F32)],
        compiler_params=_params("parallel", "arbitrary"),
        name="swiglu_ln_ple",
    )(h, wg, wu, wd, g, b, p, wpg, wpp)


def _latent_kernel(h_ref, w_ref, gkv_ref, gq_ref, tab_ref, ckv_ref, cq_ref, kr_ref):
    r = jnp.dot(h_ref[...].astype(BF16), w_ref[...], preferred_element_type=F32)
    ckv_ref[...] = _rms_norm(r[:, :KV_RANK], gkv_ref[...]).astype(BF16)
    cq_ref[...] = _rms_norm(r[:, KV_RANK:KV_RANK + Q_RANK], gq_ref[...]).astype(BF16)
    t = r[:, KV_RANK + Q_RANK:] * tab_ref[...]
    rot = t + pltpu.roll(t, ROPE_DIM, axis=1)
    lane = lax.broadcasted_iota(jnp.int32, rot.shape, 1)
    kr_ref[...] = jnp.where(lane < ROPE_DIM, rot, 0.0).astype(BF16)


def _latent(h, w_cat, g_kv, g_q, table, tm=512):
    n = w_cat.shape[1]
    return pl.pallas_call(
        _latent_kernel,
        grid=(TOKENS // tm,),
        in_specs=[pl.BlockSpec((tm, D_MODEL), lambda i: (i, 0)),
                  pl.BlockSpec((D_MODEL, n), lambda i: (0, 0)),
                  pl.BlockSpec((1, KV_RANK), lambda i: (0, 0)),
                  pl.BlockSpec((1, Q_RANK), lambda i: (0, 0)),
                  pl.BlockSpec((tm, 128), lambda i: (i, 0))],
        out_specs=[pl.BlockSpec((tm, KV_RANK), lambda i: (i, 0)),
                   pl.BlockSpec((tm, Q_RANK), lambda i: (i, 0)),
                   pl.BlockSpec((tm, 128), lambda i: (i, 0))],
        out_shape=[jax.ShapeDtypeStruct((TOKENS, KV_RANK), BF16),
                   jax.ShapeDtypeStruct((TOKENS, Q_RANK), BF16),
                   jax.ShapeDtypeStruct((TOKENS, 128), BF16)],
        compiler_params=_params("parallel"),
        name="latent",
    )(h, w_cat, g_kv, g_q, table)


def _kv_up_kernel(c_ref, wk_ref, wv_ref, kr_ref, k_ref, v_ref):
    c = c_ref[...]
    kn = jnp.dot(c, wk_ref[...], preferred_element_type=F32).astype(BF16)
    v_ref[...] = jnp.dot(c, wv_ref[...], preferred_element_type=F32).astype(BF16)
    kr = kr_ref[...]
    for h in range(MLA_HEADS):
        k_ref[:, h * HEAD_PAD:h * HEAD_PAD + NOPE_DIM] = kn[:, h * NOPE_DIM:(h + 1) * NOPE_DIM]
        k_ref[:, h * HEAD_PAD + NOPE_DIM:(h + 1) * HEAD_PAD] = kr


def _kv_up(c_kv, w_uk, w_uv, k_rope, tm=512):
    return pl.pallas_call(
        _kv_up_kernel,
        grid=(TOKENS // tm,),
        in_specs=[pl.BlockSpec((tm, KV_RANK), lambda i: (i, 0)),
                  pl.BlockSpec((KV_RANK, MLA_HEADS * NOPE_DIM), lambda i: (0, 0)),
                  pl.BlockSpec((KV_RANK, MLA_HEADS * V_DIM), lambda i: (0, 0)),
                  pl.BlockSpec((tm, 128), lambda i: (i, 0))],
        out_specs=[pl.BlockSpec((tm, MLA_HEADS * HEAD_PAD), lambda i: (i, 0)),
                   pl.BlockSpec((tm, MLA_HEADS * V_DIM), lambda i: (i, 0))],
        out_shape=[jax.ShapeDtypeStruct((TOKENS, MLA_HEADS * HEAD_PAD), BF16),
                   jax.ShapeDtypeStruct((TOKENS, MLA_HEADS * V_DIM), BF16)],
        compiler_params=_params("parallel"),
        name="kv_up",
    )(c_kv, w_uk, w_uv, k_rope)


def _q_up_kernel(c_ref, w_ref, tab_ref, q_ref):
    scale = (NOPE_DIM + ROPE_DIM) ** -0.5
    r = jnp.dot(c_ref[...], w_ref[...], preferred_element_type=F32)
    tab = tab_ref[...]
    for h in range(MLA_HEADS):
        base = h * HEAD_PAD
        q_ref[:, base:base + NOPE_DIM] = (r[:, base:base + NOPE_DIM] * scale).astype(BF16)
        t = r[:, base + NOPE_DIM:base + HEAD_PAD] * tab
        rot = (t + pltpu.roll(t, ROPE_DIM, axis=1)) * scale
        q_ref[:, base + NOPE_DIM:base + HEAD_PAD] = rot.astype(BF16)


def _q_up(c_q, w_q, table, tm=512):
    return pl.pallas_call(
        _q_up_kernel,
        grid=(TOKENS // tm,),
        in_specs=[pl.BlockSpec((tm, Q_RANK), lambda i: (i, 0)),
                  pl.BlockSpec((Q_RANK, MLA_HEADS * HEAD_PAD), lambda i: (0, 0)),
                  pl.BlockSpec((tm, 128), lambda i: (i, 0))],
        out_specs=pl.BlockSpec((tm, MLA_HEADS * HEAD_PAD), lambda i: (i, 0)),
        out_shape=jax.ShapeDtypeStruct((TOKENS, MLA_HEADS * HEAD_PAD), BF16),
        compiler_params=_params("parallel"),
        name="q_up",
    )(c_q, w_q, table)


ATT_TQ = 512
ATT_TK = 512


def _attn_kernel(q_ref, k_ref, v_ref, o_ref):
    row = lax.broadcasted_iota(jnp.int32, (ATT_TQ, ATT_TK), 0)
    col = lax.broadcasted_iota(jnp.int32, (ATT_TQ, ATT_TK), 1)
    causal = row >= col
    for qi in range(SEQ // ATT_TQ):
        q = q_ref[qi * ATT_TQ:(qi + 1) * ATT_TQ, :]
        m = l = acc = None
        for kj in range(qi + 1):
            k = k_ref[kj * ATT_TK:(kj + 1) * ATT_TK, :]
            v = v_ref[kj * ATT_TK:(kj + 1) * ATT_TK, :]
            s = lax.dot_general(q, k, (((1,), (1,)), ((), ())), preferred_element_type=F32)
            if kj == qi:
                s = jnp.where(causal, s, -jnp.inf)
            s_max = jnp.max(s, axis=-1, keepdims=True)
            if kj == 0:
                m = s_max
                p = jnp.exp(s - m)
                l = jnp.sum(p, axis=-1, keepdims=True)
                acc = jnp.dot(p.astype(BF16), v, preferred_element_type=F32)
            else:
                m_new = jnp.maximum(m, s_max)
                alpha = jnp.exp(m - m_new)
                p = jnp.exp(s - m_new)
                l = alpha * l + jnp.sum(p, axis=-1, keepdims=True)
                acc = alpha * acc + jnp.dot(p.astype(BF16), v, preferred_element_type=F32)
                m = m_new
        o_ref[qi * ATT_TQ:(qi + 1) * ATT_TQ, :] = (acc / l).astype(o_ref.dtype)


def _attention(q_cat, k_cat, v):
    return pl.pallas_call(
        _attn_kernel,
        grid=(BATCH, MLA_HEADS),
        in_specs=[pl.BlockSpec((SEQ, HEAD_PAD), lambda b, h: (b, h)),
                  pl.BlockSpec((SEQ, HEAD_PAD), lambda b, h: (b, h)),
                  pl.BlockSpec((SEQ, V_DIM), lambda b, h: (b, h))],
        out_specs=pl.BlockSpec((SEQ, V_DIM), lambda b, h: (b, h)),
        out_shape=jax.ShapeDtypeStruct((TOKENS, MLA_HEADS * V_DIM), BF16),
        compiler_params=_params("parallel", "parallel"),
        name="attention",
    )(q_cat, k_cat, v)


def _router_kernel(h_ref, w_hi_ref, w_lo_ref, b_ref, tri_ref, o_ref, cnt_ref, carry_ref):
    @pl.when(pl.program_id(0) == 0)
    def _():
        carry_ref[...] = jnp.zeros_like(carry_ref)

    logits = _dot_split(h_ref[...], w_hi_ref, w_lo_ref) + b_ref[...]
    lane = lax.broadcasted_iota(jnp.int32, logits.shape, 1).astype(F32)
    logits = jnp.where(lane < N_EXPERTS, logits, -jnp.inf)
    m1 = jnp.max(logits, axis=-1, keepdims=True)
    i1 = jnp.min(jnp.where(logits == m1, lane, 128.0), axis=-1, keepdims=True)
    rest = jnp.where(lane == i1, -jnp.inf, logits)
    m2 = jnp.max(rest, axis=-1, keepdims=True)
    i2 = jnp.min(jnp.where(rest == m2, lane, 128.0), axis=-1, keepdims=True)
    e2 = jnp.exp(m2 - m1)
    w1 = 1.0 / (1.0 + e2)
    w2 = e2 / (1.0 + e2)
    first = lane == i1
    second = lane == i2
    chosen = jnp.where(jnp.logical_or(first, second), 1.0, 0.0)
    before = carry_ref[0:1, :] + jnp.dot(tri_ref[...], chosen.astype(BF16),
                                         preferred_element_type=F32)
    r1 = jnp.sum(jnp.where(first, before, 0.0), axis=-1, keepdims=True)
    r2 = jnp.sum(jnp.where(second, before, 0.0), axis=-1, keepdims=True)
    total = carry_ref[0:1, :] + jnp.sum(chosen, axis=0, keepdims=True)
    carry_ref[...] = jnp.broadcast_to(total, carry_ref.shape)
    cnt_ref[...] = jnp.broadcast_to(total, cnt_ref.shape)
    out = jnp.where(lane == 0.0, i1, jnp.where(lane == 1.0, i2, 0.0))
    out = jnp.where(lane == 2.0, w1, jnp.where(lane == 3.0, w2, out))
    out = jnp.where(lane == 4.0, r1, jnp.where(lane == 5.0, r2, out))
    o_ref[...] = out


def _router(h, w_router, b_router, tm=512):
    w_hi, w_lo = _split_weight(jnp.pad(w_router, ((0, 0), (0, 128 - N_EXPERTS))))
    b_pad = jnp.pad(b_router, ((0, 0), (0, 128 - N_EXPERTS)))
    tri = jnp.tril(jnp.ones((tm, tm), BF16), k=-1)
    return pl.pallas_call(
        _router_kernel,
        grid=(TOKENS // tm,),
        in_specs=[pl.BlockSpec((tm, D_MODEL), lambda i: (i, 0)),
                  pl.BlockSpec((D_MODEL, 128), lambda i: (0, 0)),
                  pl.BlockSpec((D_MODEL, 128), lambda i: (0, 0)),
                  pl.BlockSpec((1, 128), lambda i: (0, 0)),
                  pl.BlockSpec((tm, tm), lambda i: (0, 0))],
        out_specs=[pl.BlockSpec((tm, 128), lambda i: (i, 0)),
                   pl.BlockSpec((8, 128), lambda i: (0, 0))],
        out_shape=[jax.ShapeDtypeStruct((TOKENS, 128), F32),
                   jax.ShapeDtypeStruct((8, 128), F32)],
        scratch_shapes=[pltpu.VMEM((8, 128), F32)],
        compiler_params=_params("arbitrary"),
        name="router",
    )(h, w_hi, w_lo, b_pad, tri)


MOE_TM = 1024
MOE_TF = 256
MOE_ROWS = 2 * TOKENS + N_EXPERTS * MOE_TM
MOE_TILES = MOE_ROWS // MOE_TM
MOE_NF = D_FF_EXPERT // MOE_TF
MOE_ISSUE = 96
MOE_BUF_ROWS = MOE_NF * MOE_ISSUE
assert MOE_BUF_ROWS >= MOE_TM
PACKED = D_MODEL // 2


def _moe_kernel(te_ref, nt_ref, src_ref, h_hbm, wg_ref, wu_ref, wd_ref, o_ref,
                xbuf, x_bf, acc_ref, sem):
    i = pl.program_id(0)
    f = pl.program_id(1)
    last = pl.num_programs(1) - 1
    n_valid = nt_ref[0]
    valid = i < n_valid
    slot = lax.rem(i, 2)

    def row_copy(tile, r, s):
        tok = src_ref[jnp.minimum(tile, MOE_TILES - 1) * MOE_TM + jnp.minimum(r, MOE_TM - 1)]
        return pltpu.make_async_copy(h_hbm.at[pl.ds(tok, 1), :], xbuf.at[s, pl.ds(r, 1), :],
                                     sem.at[s])

    def wait_rows(s):
        pltpu.make_async_copy(xbuf.at[s], xbuf.at[s], sem.at[s]).wait()

    @pl.when(jnp.logical_and(i == 0, f == 0))
    def _():
        def body(r, c):
            row_copy(0, r, 0).start()
            return c
        lax.fori_loop(0, MOE_BUF_ROWS, body, 0, unroll=8)

    @pl.when(jnp.logical_and(f == 0, i <= n_valid))
    def _():
        wait_rows(slot)

    @pl.when(valid)
    def _():
        @pl.when(f == 0)
        def _():
            lo, hi = _unpack_bf16_halves(xbuf[slot, 0:MOE_TM, :])
            x_bf[:, :PACKED] = lo
            x_bf[:, PACKED:] = hi
            acc_ref[...] = jnp.zeros_like(acc_ref)

        for j in range(MOE_ISSUE):
            row_copy(i + 1, f * MOE_ISSUE + j, 1 - slot).start()

        a = x_bf[...]
        gate = jnp.dot(a, wg_ref[0].astype(BF16), preferred_element_type=F32)
        up = jnp.dot(a, wu_ref[0].astype(BF16), preferred_element_type=F32)
        mid = (_silu(gate) * up).astype(BF16)
        acc_ref[...] += jnp.dot(mid, wd_ref[0].astype(BF16), preferred_element_type=F32)

        @pl.when(f == last)
        def _():
            o_ref[...] = acc_ref[...].astype(o_ref.dtype)

        @pl.when(jnp.logical_and(f == last, i == MOE_TILES - 1))
        def _():
            wait_rows(1 - slot)

    @pl.when(jnp.logical_and(jnp.logical_not(valid), f == last))
    def _():
        o_ref[...] = jnp.zeros_like(o_ref)


def _moe_experts(tile_expert, n_tiles, src, h_packed, wg, wu, wd):
    def f_of(i, f, nt):
        return jnp.where(i < nt[0], f, MOE_NF - 1)

    grid_spec = pltpu.PrefetchScalarGridSpec(
        num_scalar_prefetch=3,
        grid=(MOE_TILES, MOE_NF),
        in_specs=[
            pl.BlockSpec(memory_space=pl.ANY),
            pl.BlockSpec((1, D_MODEL, MOE_TF),
                         lambda i, f, te, nt, src: (te[i], 0, f_of(i, f, nt))),
            pl.BlockSpec((1, D_MODEL, MOE_TF),
                         lambda i, f, te, nt, src: (te[i], 0, f_of(i, f, nt))),
            pl.BlockSpec((1, MOE_TF, D_MODEL),
                         lambda i, f, te, nt, src: (te[i], f_of(i, f, nt), 0)),
        ],
        out_specs=pl.BlockSpec((MOE_TM, D_MODEL), lambda i, f, te, nt, src: (i, 0)),
        scratch_shapes=[pltpu.VMEM((2, MOE_BUF_ROWS, PACKED), U32),
                        pltpu.VMEM((MOE_TM, D_MODEL), BF16),
                        pltpu.VMEM((MOE_TM, D_MODEL), F32),
                        pltpu.SemaphoreType.DMA((2,))],
    )
    return pl.pallas_call(
        _moe_kernel,
        grid_spec=grid_spec,
        out_shape=jax.ShapeDtypeStruct((MOE_ROWS, D_MODEL), BF16),
        compiler_params=_params("arbitrary", "arbitrary"),
        name="moe_experts",
    )(tile_expert, n_tiles, src, h_packed, wg, wu, wd)


def _combine_ln_kernel(h_ref, y0_ref, y1_ref, w0_ref, w1_ref, g_ref, b_ref, p_ref, wpg_ref,
                       wpp_ref, o_ref):
    ff = w0_ref[...] * y0_ref[...].astype(F32) + w1_ref[...] * y1_ref[...].astype(F32)
    h2 = _layer_norm(DEEPNORM_ALPHA * h_ref[...] + ff, g_ref[...], b_ref[...])
    o_ref[...] = _add_ple(h2, p_ref, wpg_ref, wpp_ref)


def _combine_ln_ple(h, y0, y1, w0, w1, g, b, p, wpg, wpp, tm=512):
    row = pl.BlockSpec((tm, D_MODEL), lambda i: (i, 0))
    col = pl.BlockSpec((tm, 1), lambda i: (i, 0))
    return pl.pallas_call(
        _combine_ln_kernel,
        grid=(TOKENS // tm,),
        in_specs=[row, row, row, col, col, _resident((1, D_MODEL)), _resident((1, D_MODEL)),
                  pl.BlockSpec((tm, PLE_DIM), lambda i: (i, 0)),
                  _resident((D_MODEL, D_MODEL)), _resident((PLE_DIM, D_MODEL))],
        out_specs=row,
        out_shape=jax.ShapeDtypeStruct((TOKENS, D_MODEL), F32),
        compiler_params=_params("parallel"),
        name="combine_ln_ple",
    )(h, y0, y1, w0, w1, g, b, p, wpg, wpp)


def _swap_halves(w):
    half = w.shape[-1] // 2
    return jnp.concatenate([w[..., half:], w[..., :half]], axis=-1)


def kernel(x, p, positions, ssm_w_in, ssm_conv_w, ssm_conv_b, ssm_dt_bias, ssm_a_log, ssm_d, ssm_norm_g, ssm_w_out, kv_w_down, kv_norm_g, kv_w_rope, kv_w_uk, kv_w_uv, mla_w_dq, mla_q_norm_g, mla_w_uq, mla_w_o, ffn_w_gate, ffn_w_up, ffn_w_down, moe_w_router, moe_b_router, moe_w_gate, moe_w_up, moe_w_down, ln1_g, ln1_b, ln2_g, ln2_b, ple_w_proj, ple_w_gate):
    h0 = x.reshape(TOKENS, D_MODEL)
    p = p.reshape(DEPTH, TOKENS, PLE_DIM)
    row = lambda v: v.reshape(1, -1)

    zx_cols = D_INNER + CONV_DIM
    w_dt = ssm_w_in[0][:, zx_cols:]
    zx = _matmul(h0, ssm_w_in[0].astype(BF16), zx_cols, BF16, tm=1024, tn=1024)
    parts, acum = _dt_prep(h0, w_dt, row(ssm_dt_bias[0]), row(ssm_a_log[0]))
    xbc = _conv_silu(zx, ssm_conv_w[0], row(ssm_conv_b[0]))
    parts = parts.reshape(N_WIDE * N_SPLIT, BATCH, SEQ, SSM_GROUPS, HEADS_PER_GROUP)
    parts = parts.transpose(1, 3, 2, 0, 4).reshape(BATCH, SSM_GROUPS, SEQ, WIDE_ROWS)
    acumt = acum.reshape(BATCH, N_CHUNKS, SSM_CHUNK, SSM_GROUPS, HEADS_PER_GROUP)
    acumt = acumt.transpose(0, 3, 1, 4, 2)
    d_full = row(jnp.repeat(ssm_d[0], SSM_HEAD_DIM))
    y = _ssd(xbc, zx, parts, acumt, d_full, row(ssm_norm_g[0]))
    (h,) = _matmul_ln(y, ssm_w_out[0].astype(BF16), h0, row(ln1_g[0]), row(ln1_b[0]),
                      also_packed=False)

    h = _swiglu_ln_ple(h, ffn_w_gate[0].astype(BF16), ffn_w_up[0].astype(BF16),
                       ffn_w_down[0].astype(BF16), row(ln2_g[0]), row(ln2_b[0]),
                       p[0], ple_w_gate[0].astype(BF16), ple_w_proj[0].astype(BF16))

    inv_freq = ROPE_THETA ** (-jnp.arange(0, ROPE_DIM, 2, dtype=F32) / ROPE_DIM)
    table = _rope_table(positions.reshape(TOKENS, 1), row(jnp.tile(inv_freq, 4)))
    w_lat = jnp.concatenate([kv_w_down, mla_w_dq[0], kv_w_rope, _swap_halves(kv_w_rope)],
                            axis=1).astype(BF16)
    c_kv, c_q, k_rope = _latent(h, w_lat, row(kv_norm_g), row(mla_q_norm_g[0]), table)
    k_cat, v = _kv_up(c_kv, kv_w_uk.astype(BF16), kv_w_uv.astype(BF16), k_rope)
    w_uq = mla_w_uq[0].reshape(Q_RANK, MLA_HEADS, NOPE_DIM + ROPE_DIM)
    w_q = jnp.concatenate([w_uq, _swap_halves(w_uq[..., NOPE_DIM:])], axis=-1)
    w_q = w_q.reshape(Q_RANK, MLA_HEADS * HEAD_PAD).astype(BF16)
    q_cat = _q_up(c_q, w_q, table)
    o = _attention(q_cat, k_cat, v)
    h, h_packed = _matmul_ln(o, mla_w_o[0].astype(BF16), h, row(ln1_g[1]), row(ln1_b[1]),
                             also_packed=True)

    route, counts = _router(h, moe_w_router[0], row(moe_b_router[0]))
    counts = counts[0, :N_EXPERTS].astype(jnp.int32)
    tiles_per = (counts + MOE_TM - 1) // MOE_TM
    tile_end = jnp.cumsum(tiles_per)
    start = (tile_end - tiles_per) * MOE_TM
    expert = route[:, 0:2].astype(jnp.int32)
    rank = route[:, 4:6].astype(jnp.int32)
    start_of = jnp.sum(jnp.where(expert[..., None] == jnp.arange(N_EXPERTS, dtype=jnp.int32),
                                 start, 0), axis=-1)
    pos = start_of + rank
    n_tiles = tile_end[-1:].astype(jnp.int32)
    tile_ids = jnp.minimum(jnp.arange(MOE_TILES, dtype=jnp.int32), n_tiles[0] - 1)
    tile_expert = jnp.searchsorted(tile_end, tile_ids, side="right").astype(jnp.int32)
    token = jnp.broadcast_to(jnp.arange(TOKENS, dtype=jnp.int32)[:, None], (TOKENS, 2))
    src = jnp.zeros((MOE_ROWS,), jnp.int32).at[pos.reshape(-1)].set(
        token.reshape(-1), unique_indices=True)
    y_sorted = _moe_experts(tile_expert, n_tiles, src, h_packed, moe_w_gate[0], moe_w_up[0],
                            moe_w_down[0])
    y0 = y_sorted.at[pos[:, 0]].get(mode="promise_in_bounds")
    y1 = y_sorted.at[pos[:, 1]].get(mode="promise_in_bounds")
    h = _combine_ln_ple(h, y0, y1, route[:, 2:3], route[:, 3:4], row(ln2_g[1]), row(ln2_b[1]),
                        p[1], ple_w_gate[1].astype(BF16), ple_w_proj[1].astype(BF16))
    return h.reshape(BATCH, SEQ, D_MODEL)
```

```python
import functools

import jax
import jax.numpy as jnp
from jax import lax
from jax.experimental import pallas as pl
from jax.experimental.pallas import tpu as pltpu

F32 = jnp.float32
BF16 = jnp.bfloat16
U32 = jnp.uint32
HIGHEST = lax.Precision.HIGHEST

D_MODEL = 2048
BATCH = 16
SEQ = 2048
TOKENS = BATCH * SEQ
DEPTH = 2

D_INNER = 4096
SSM_HEAD_DIM = 64
SSM_HEADS = 64
SSM_GROUPS = 8
HEADS_PER_GROUP = 8
SSM_STATE = 128
GROUP_WIDTH = HEADS_PER_GROUP * SSM_HEAD_DIM
CONV_K = 4
CONV_DIM = D_INNER + 2 * SSM_GROUPS * SSM_STATE
SSM_CHUNK = 128
N_CHUNKS = SEQ // SSM_CHUNK

MLA_HEADS = 16
Q_RANK = 512
KV_RANK = 512
NOPE_DIM = 128
ROPE_DIM = 64
V_DIM = 128
ROPE_THETA = 10000.0
HEAD_PAD = 256

D_FF = 5632
N_EXPERTS = 8
D_FF_EXPERT = 2816

PLE_DIM = 256
DEEPNORM_ALPHA = (2.0 * DEPTH) ** 0.25
LN_EPS = 1e-5
RMS_EPS = 1e-6

VMEM_LIMIT = 56 * 1024 * 1024


def _params(*sem):
    return pltpu.CompilerParams(dimension_semantics=sem, vmem_limit_bytes=VMEM_LIMIT)


def _layer_norm(x, g, b):
    mu = jnp.mean(x, axis=-1, keepdims=True)
    xc = x - mu
    var = jnp.mean(xc * xc, axis=-1, keepdims=True)
    return xc * lax.rsqrt(var + LN_EPS) * g + b


def _rms_norm(x, g):
    return x * lax.rsqrt(jnp.mean(x * x, axis=-1, keepdims=True) + RMS_EPS) * g


def _silu(x):
    return x * (1.0 / (1.0 + jnp.exp(-x)))


def _sigmoid(x):
    return 1.0 / (1.0 + jnp.exp(-x))


def _rope_table_kernel(pos_ref, freq_ref, o_ref):
    ang = pos_ref[...].astype(F32) * freq_ref[...]
    lane = lax.broadcasted_iota(jnp.int32, ang.shape, 1)
    c = jnp.cos(ang)
    s = jnp.sin(ang)
    o_ref[...] = jnp.where(lane < 64, c, jnp.where(lane < 96, -s, s))


def _rope_table(pos_col, freq_row, tm=1024):
    return pl.pallas_call(
        _rope_table_kernel,
        grid=(TOKENS // tm,),
        in_specs=[pl.BlockSpec((tm, 1), lambda i: (i, 0)),
                  pl.BlockSpec((1, 128), lambda i: (0, 0))],
        out_specs=pl.BlockSpec((tm, 128), lambda i: (i, 0)),
        out_shape=jax.ShapeDtypeStruct((TOKENS, 128), F32),
        compiler_params=_params("parallel"),
        name="rope_table",
    )(pos_col, freq_row)


def _matmul_kernel(a_ref, w_ref, o_ref, a_bf):
    @pl.when(pl.program_id(1) == 0)
    def _():
        a_bf[...] = a_ref[...].astype(BF16)

    o_ref[...] = jnp.dot(a_bf[...], w_ref[...], preferred_element_type=F32).astype(o_ref.dtype)


def _matmul(a, w, n, out_dtype, tm, tn):
    m, k = a.shape
    return pl.pallas_call(
        _matmul_kernel,
        grid=(m // tm, n // tn),
        in_specs=[pl.BlockSpec((tm, k), lambda i, j: (i, 0)),
                  pl.BlockSpec((k, tn), lambda i, j: (0, j))],
        out_specs=pl.BlockSpec((tm, tn), lambda i, j: (i, j)),
        out_shape=jax.ShapeDtypeStruct((m, n), out_dtype),
        scratch_shapes=[pltpu.VMEM((tm, k), BF16)],
        compiler_params=_params("parallel", "arbitrary"),
        name="matmul",
    )(a, w)


N_SPLIT = 3
N_WIDE = 4


def _split_bf16(v):
    parts = []
    for _ in range(N_SPLIT):
        part = v.astype(BF16)
        parts.append(part)
        v = v - part.astype(F32)
    return parts


def _dot_split(x, w_hi_ref, w_lo_ref):
    x_hi = x.astype(BF16)
    x_lo = (x - x_hi.astype(F32)).astype(BF16)
    w_hi = w_hi_ref[...]
    return (jnp.dot(x_hi, w_hi, preferred_element_type=F32)
            + jnp.dot(x_lo, w_hi, preferred_element_type=F32)
            + jnp.dot(x_hi, w_lo_ref[...], preferred_element_type=F32))


def _split_weight(w):
    w_hi = w.astype(BF16)
    return w_hi, (w - w_hi.astype(F32)).astype(BF16)


def _dt_kernel(x_ref, w_hi_ref, w_lo_ref, bias_ref, alog_ref, parts_ref, acum_ref, *, tm):
    raw = _dot_split(x_ref[...], w_hi_ref, w_lo_ref) + bias_ref[...]
    dt = jnp.maximum(raw, 0.0) + jnp.log1p(jnp.exp(-jnp.abs(raw)))
    a = -jnp.exp(alog_ref[...]) * dt
    row = lax.broadcasted_iota(jnp.int32, (SSM_CHUNK, SSM_CHUNK), 0)
    col = lax.broadcasted_iota(jnp.int32, (SSM_CHUNK, SSM_CHUNK), 1)
    tril = jnp.where(row >= col, 1.0, 0.0).astype(F32)
    for c in range(tm // SSM_CHUNK):
        sl = slice(c * SSM_CHUNK, (c + 1) * SSM_CHUNK)
        ac = jnp.dot(tril, a[sl, :], preferred_element_type=F32, precision=HIGHEST)
        acum_ref[sl, :] = ac
        ac_end = ac[SSM_CHUNK - 1:SSM_CHUNK, :]
        for qty, val in enumerate((dt[sl, :], jnp.exp(ac), jnp.exp(ac_end - ac), ac)):
            for part, piece in enumerate(_split_bf16(val)):
                parts_ref[qty * N_SPLIT + part, sl, :] = piece


def _dt_prep(x, w_dt, dt_bias, a_log, tm=512):
    n_parts = N_WIDE * N_SPLIT
    w_hi, w_lo = _split_weight(w_dt)
    return pl.pallas_call(
        functools.partial(_dt_kernel, tm=tm),
        grid=(TOKENS // tm,),
        in_specs=[pl.BlockSpec((tm, D_MODEL), lambda i: (i, 0)),
                  pl.BlockSpec((D_MODEL, SSM_HEADS), lambda i: (0, 0)),
                  pl.BlockSpec((D_MODEL, SSM_HEADS), lambda i: (0, 0)),
                  pl.BlockSpec((1, SSM_HEADS), lambda i: (0, 0)),
                  pl.BlockSpec((1, SSM_HEADS), lambda i: (0, 0))],
        out_specs=[pl.BlockSpec((n_parts, tm, SSM_HEADS), lambda i: (0, i, 0)),
                   pl.BlockSpec((tm, SSM_HEADS), lambda i: (i, 0))],
        out_shape=[jax.ShapeDtypeStruct((n_parts, TOKENS, SSM_HEADS), BF16),
                   jax.ShapeDtypeStruct((TOKENS, SSM_HEADS), F32)],
        compiler_params=_params("parallel"),
        name="dt_prep",
    )(x, w_hi, w_lo, dt_bias, a_log)


CONV_ROWS = 256
CONV_HALO = 16


def _conv_kernel(u_ref, w_ref, b_ref, o_ref):
    w = w_ref[...]
    b = b_ref[...]
    tc = u_ref.shape[1]
    for i in range(SEQ // CONV_ROWS):
        r = i * CONV_ROWS
        cur = u_ref[r:r + CONV_ROWS, :].astype(F32)
        if i == 0:
            halo = jnp.zeros((CONV_HALO, tc), F32)
        else:
            halo = u_ref[r - CONV_HALO:r, :].astype(F32)
        win = jnp.concatenate([halo, cur], axis=0)
        acc = b + w[CONV_K - 1:CONV_K, :] * cur
        for k in range(CONV_K - 1):
            shifted = pltpu.roll(win, CONV_K - 1 - k, axis=0)[CONV_HALO:, :]
            acc = acc + w[k:k + 1, :] * shifted
        o_ref[r:r + CONV_ROWS, :] = _silu(acc).astype(o_ref.dtype)


def _conv_silu(zx, conv_w, conv_b, tc=512):
    col0 = D_INNER // tc
    return pl.pallas_call(
        _conv_kernel,
        grid=(BATCH, CONV_DIM // tc),
        in_specs=[pl.BlockSpec((SEQ, tc), lambda b, j: (b, col0 + j)),
                  pl.BlockSpec((CONV_K, tc), lambda b, j: (0, j)),
                  pl.BlockSpec((1, tc), lambda b, j: (0, j))],
        out_specs=pl.BlockSpec((SEQ, tc), lambda b, j: (b, j)),
        out_shape=jax.ShapeDtypeStruct((TOKENS, CONV_DIM), BF16),
        compiler_params=_params("parallel", "parallel"),
        name="conv_silu",
    )(zx, conv_w, conv_b)


WIDE_ROWS = N_WIDE * N_SPLIT * HEADS_PER_GROUP
WIDE_ACUM = 3 * GROUP_WIDTH
WIDE_COLS = WIDE_ACUM + HEADS_PER_GROUP * SSM_CHUNK


def _widen_matrix():
    r = jnp.arange(WIDE_ROWS)[:, None]
    c = jnp.arange(WIDE_COLS)[None, :]
    qty = r // (N_SPLIT * HEADS_PER_GROUP)
    head = r % HEADS_PER_GROUP
    chan = (c // GROUP_WIDTH == qty) & ((c % GROUP_WIDTH) // SSM_HEAD_DIM == head) & (c < WIDE_ACUM)
    copy = (qty == N_WIDE - 1) & (c >= WIDE_ACUM) & ((c - WIDE_ACUM) // SSM_CHUNK == head)
    return (chan | copy).astype(BF16)


def _ssd_kernel(x_ref, b_ref, c_ref, z_ref, parts_ref, acumt_ref, widen_ref, d_ref, g_ref,
                o_ref, state_ref):
    q = SSM_CHUNK
    state_ref[...] = jnp.zeros_like(state_ref)
    trow = lax.broadcasted_iota(jnp.int32, (q, q), 0)
    tcol = lax.broadcasted_iota(jnp.int32, (q, q), 1)
    causal = trow >= tcol
    first_head = lax.broadcasted_iota(jnp.int32, (q, 2 * SSM_HEAD_DIM), 1) < SSM_HEAD_DIM

    def chunk(c, carry):
        rows = pl.ds(pl.multiple_of(c * q, q), q)
        xf = x_ref[rows, :].astype(F32)
        bm = b_ref[rows, :]
        cm = c_ref[rows, :]
        act = acumt_ref[0, 0, c]
        wide = jnp.dot(parts_ref[0, 0, rows, :], widen_ref[...], preferred_element_type=F32)
        xdt = xf * wide[:, :GROUP_WIDTH]
        decay_out = wide[:, GROUP_WIDTH:2 * GROUP_WIDTH]
        to_end = wide[:, 2 * GROUP_WIDTH:WIDE_ACUM]
        xdt_bf = xdt.astype(BF16)

        cb = lax.dot_general(cm, bm, (((1,), (1,)), ((), ())), preferred_element_type=F32)
        state = state_ref[...]
        y = jnp.dot(cm, state.astype(BF16), preferred_element_type=F32) * decay_out

        pairs = []
        for pr in range(HEADS_PER_GROUP // 2):
            xpair = xdt_bf[:, pr * 128:(pr + 1) * 128]
            ms = []
            for j in (2 * pr, 2 * pr + 1):
                col0 = WIDE_ACUM + j * q
                seg = jnp.where(causal, wide[:, col0:col0 + q] - act[j:j + 1, :], -jnp.inf)
                ms.append((cb * jnp.exp(seg)).astype(BF16))
            zero = jnp.zeros_like(xpair)
            rhs = jnp.concatenate([jnp.where(first_head, xpair, zero),
                                   jnp.where(first_head, zero, xpair)], axis=0)
            pairs.append(jnp.dot(jnp.concatenate(ms, axis=1), rhs, preferred_element_type=F32))
        y = y + jnp.concatenate(pairs, axis=1)

        upd = lax.dot_general(bm, (xdt * to_end).astype(BF16), (((0,), (0,)), ((), ())),
                              preferred_element_type=F32)
        state_ref[...] = state * decay_out[q - 1:q, :] + upd

        y = y + d_ref[...] * xf
        y = y * _silu(z_ref[rows, :].astype(F32))
        o_ref[rows, :] = _rms_norm(y, g_ref[...]).astype(o_ref.dtype)
        return carry

    lax.fori_loop(0, N_CHUNKS, chunk, 0, unroll=2)


def _ssd(xbc, zx, parts, acumt, d_full, norm_g):
    x_cols = D_INNER // SSM_STATE
    nb = SSM_GROUPS
    return pl.pallas_call(
        _ssd_kernel,
        grid=(BATCH, SSM_GROUPS),
        in_specs=[
            pl.BlockSpec((SEQ, GROUP_WIDTH), lambda b, g: (b, g)),
            pl.BlockSpec((SEQ, SSM_STATE), lambda b, g: (b, x_cols + g)),
            pl.BlockSpec((SEQ, SSM_STATE), lambda b, g: (b, x_cols + nb + g)),
            pl.BlockSpec((SEQ, GROUP_WIDTH), lambda b, g: (b, g)),
            pl.BlockSpec((1, 1, SEQ, WIDE_ROWS), lambda b, g: (b, g, 0, 0)),
            pl.BlockSpec((1, 1, N_CHUNKS, HEADS_PER_GROUP, SSM_CHUNK), lambda b, g: (b, g, 0, 0, 0)),
            pl.BlockSpec((WIDE_ROWS, WIDE_COLS), lambda b, g: (0, 0)),
            pl.BlockSpec((1, GROUP_WIDTH), lambda b, g: (0, g)),
            pl.BlockSpec((1, GROUP_WIDTH), lambda b, g: (0, g)),
        ],
        out_specs=pl.BlockSpec((SEQ, GROUP_WIDTH), lambda b, g: (b, g)),
        out_shape=jax.ShapeDtypeStruct((TOKENS, D_INNER), BF16),
        scratch_shapes=[pltpu.VMEM((SSM_STATE, GROUP_WIDTH), F32)],
        compiler_params=_params("parallel", "parallel"),
        name="ssd_scan",
    )(xbc, xbc, xbc, zx, parts, acumt, _widen_matrix(), d_full, norm_g)


HIGH_HALF = 0xFFFF0000


def _pack_bf16_halves(x):
    half = x.shape[1] // 2
    lo = pltpu.bitcast(x[:, :half].astype(BF16).astype(F32), U32)
    hi = pltpu.bitcast(x[:, half:].astype(BF16).astype(F32), U32)
    return lax.shift_right_logical(lo, U32(16)) | (hi & U32(HIGH_HALF))


def _unpack_bf16_halves(words):
    lo = pltpu.bitcast(lax.shift_left(words, U32(16)), F32)
    hi = pltpu.bitcast(words & U32(HIGH_HALF), F32)
    return lo.astype(BF16), hi.astype(BF16)


def _matmul_ln_kernel(a_ref, w_ref, res_ref, g_ref, b_ref, o_ref, *packed_ref):
    mix = jnp.dot(a_ref[...], w_ref[...], preferred_element_type=F32)
    out = _layer_norm(DEEPNORM_ALPHA * res_ref[...] + mix, g_ref[...], b_ref[...])
    o_ref[...] = out
    for ref in packed_ref:
        ref[...] = _pack_bf16_halves(out)


def _resident(shape):
    return pl.BlockSpec(shape, lambda *_: (0,) * len(shape), pipeline_mode=pl.Buffered(1))


def _matmul_ln(a, w, res, g, b, also_packed, tm=512):
    m, k = a.shape
    n = w.shape[1]
    row = pl.BlockSpec((tm, n), lambda i: (i, 0))
    out_specs = [row]
    out_shape = [jax.ShapeDtypeStruct((m, n), F32)]
    if also_packed:
        out_specs.append(pl.BlockSpec((tm, n // 2), lambda i: (i, 0)))
        out_shape.append(jax.ShapeDtypeStruct((m, n // 2), U32))
    return pl.pallas_call(
        _matmul_ln_kernel,
        grid=(m // tm,),
        in_specs=[pl.BlockSpec((tm, k), lambda i: (i, 0)),
                  _resident((k, n)), row, _resident((1, n)), _resident((1, n))],
        out_specs=out_specs,
        out_shape=out_shape,
        compiler_params=_params("parallel"),
        name="matmul_ln",
    )(a, w, res, g, b)


def _add_ple(h2, p_ref, wpg_ref, wpp_ref):
    gate = jnp.dot(h2.astype(BF16), wpg_ref[...], preferred_element_type=F32)
    proj = jnp.dot(p_ref[...].astype(BF16), wpp_ref[...], preferred_element_type=F32)
    return h2 + _sigmoid(gate) * proj


def _swiglu_ln_kernel(h_ref, wg_ref, wu_ref, wd_ref, g_ref, b_ref, p_ref, wpg_ref, wpp_ref,
                      o_ref, h_bf, acc_ref):
    f = pl.program_id(1)

    @pl.when(f == 0)
    def _():
        h_bf[...] = h_ref[...].astype(BF16)
        acc_ref[...] = jnp.zeros_like(acc_ref)

    a = h_bf[...]
    gate = jnp.dot(a, wg_ref[...], preferred_element_type=F32)
    up = jnp.dot(a, wu_ref[...], preferred_element_type=F32)
    mid = (_silu(gate) * up).astype(BF16)
    acc_ref[...] += jnp.dot(mid, wd_ref[...], preferred_element_type=F32)

    @pl.when(f == pl.num_programs(1) - 1)
    def _():
        x = DEEPNORM_ALPHA * h_ref[...] + acc_ref[...]
        h2 = _layer_norm(x, g_ref[...], b_ref[...])
        o_ref[...] = _add_ple(h2, p_ref, wpg_ref, wpp_ref)


def _swiglu_ln_ple(h, wg, wu, wd, g, b, p, wpg, wpp, tm=512, tf=512):
    m = h.shape[0]
    ff = wg.shape[1]
    return pl.pallas_call(
        _swiglu_ln_kernel,
        grid=(m // tm, ff // tf),
        in_specs=[pl.BlockSpec((tm, D_MODEL), lambda i, f: (i, 0)),
                  pl.BlockSpec((D_MODEL, tf), lambda i, f: (0, f)),
                  pl.BlockSpec((D_MODEL, tf), lambda i, f: (0, f)),
                  pl.BlockSpec((tf, D_MODEL), lambda i, f: (f, 0)),
                  _resident((1, D_MODEL)), _resident((1, D_MODEL)),
                  pl.BlockSpec((tm, PLE_DIM), lambda i, f: (i, 0)),
                  _resident((D_MODEL, D_MODEL)), _resident((PLE_DIM, D_MODEL))],
        out_specs=pl.BlockSpec((tm, D_MODEL), lambda i, f: (i, 0)),
        out_shape=jax.ShapeDtypeStruct((m, D_MODEL), F32),
        scratch_shapes=[pltpu.VMEM((tm, D_MODEL), BF16), pltpu.VMEM((tm, D_MODEL), F32)],
        compiler_params=_params("parallel", "arbitrary"),
        name="swiglu_ln_ple",
    )(h, wg, wu, wd, g, b, p, wpg, wpp)


def _latent_kernel(h_ref, w_ref, gkv_ref, gq_ref, tab_ref, ckv_ref, cq_ref, kr_ref):
    r = jnp.dot(h_ref[...].astype(BF16), w_ref[...], preferred_element_type=F32)
    ckv_ref[...] = _rms_norm(r[:, :KV_RANK], gkv_ref[...]).astype(BF16)
    cq_ref[...] = _rms_norm(r[:, KV_RANK:KV_RANK + Q_RANK], gq_ref[...]).astype(BF16)
    t = r[:, KV_RANK + Q_RANK:] * tab_ref[...]
    rot = t + pltpu.roll(t, ROPE_DIM, axis=1)
    lane = lax.broadcasted_iota(jnp.int32, rot.shape, 1)
    kr_ref[...] = jnp.where(lane < ROPE_DIM, rot, 0.0).astype(BF16)


def _latent(h, w_cat, g_kv, g_q, table, tm=512):
    n = w_cat.shape[1]
    return pl.pallas_call(
        _latent_kernel,
        grid=(TOKENS // tm,),
        in_specs=[pl.BlockSpec((tm, D_MODEL), lambda i: (i, 0)),
                  pl.BlockSpec((D_MODEL, n), lambda i: (0, 0)),
                  pl.BlockSpec((1, KV_RANK), lambda i: (0, 0)),
                  pl.BlockSpec((1, Q_RANK), lambda i: (0, 0)),
                  pl.BlockSpec((tm, 128), lambda i: (i, 0))],
        out_specs=[pl.BlockSpec((tm, KV_RANK), lambda i: (i, 0)),
                   pl.BlockSpec((tm, Q_RANK), lambda i: (i, 0)),
                   pl.BlockSpec((tm, 128), lambda i: (i, 0))],
        out_shape=[jax.ShapeDtypeStruct((TOKENS, KV_RANK), BF16),
                   jax.ShapeDtypeStruct((TOKENS, Q_RANK), BF16),
                   jax.ShapeDtypeStruct((TOKENS, 128), BF16)],
        compiler_params=_params("parallel"),
        name="latent",
    )(h, w_cat, g_kv, g_q, table)


def _kv_up_kernel(c_ref, wk_ref, wv_ref, kr_ref, k_ref, v_ref):
    c = c_ref[...]
    kn = jnp.dot(c, wk_ref[...], preferred_element_type=F32).astype(BF16)
    v_ref[...] = jnp.dot(c, wv_ref[...], preferred_element_type=F32).astype(BF16)
    kr = kr_ref[...]
    for h in range(MLA_HEADS):
        k_ref[:, h * HEAD_PAD:h * HEAD_PAD + NOPE_DIM] = kn[:, h * NOPE_DIM:(h + 1) * NOPE_DIM]
        k_ref[:, h * HEAD_PAD + NOPE_DIM:(h + 1) * HEAD_PAD] = kr


def _kv_up(c_kv, w_uk, w_uv, k_rope, tm=512):
    return pl.pallas_call(
        _kv_up_kernel,
        grid=(TOKENS // tm,),
        in_specs=[pl.BlockSpec((tm, KV_RANK), lambda i: (i, 0)),
                  pl.BlockSpec((KV_RANK, MLA_HEADS * NOPE_DIM), lambda i: (0, 0)),
                  pl.BlockSpec((KV_RANK, MLA_HEADS * V_DIM), lambda i: (0, 0)),
                  pl.BlockSpec((tm, 128), lambda i: (i, 0))],
        out_specs=[pl.BlockSpec((tm, MLA_HEADS * HEAD_PAD), lambda i: (i, 0)),
                   pl.BlockSpec((tm, MLA_HEADS * V_DIM), lambda i: (i, 0))],
        out_shape=[jax.ShapeDtypeStruct((TOKENS, MLA_HEADS * HEAD_PAD), BF16),
                   jax.ShapeDtypeStruct((TOKENS, MLA_HEADS * V_DIM), BF16)],
        compiler_params=_params("parallel"),
        name="kv_up",
    )(c_kv, w_uk, w_uv, k_rope)


def _q_up_kernel(c_ref, w_ref, tab_ref, q_ref):
    scale = (NOPE_DIM + ROPE_DIM) ** -0.5
    r = jnp.dot(c_ref[...], w_ref[...], preferred_element_type=F32)
    tab = tab_ref[...]
    for h in range(MLA_HEADS):
        base = h * HEAD_PAD
        q_ref[:, base:base + NOPE_DIM] = (r[:, base:base + NOPE_DIM] * scale).astype(BF16)
        t = r[:, base + NOPE_DIM:base + HEAD_PAD] * tab
        rot = (t + pltpu.roll(t, ROPE_DIM, axis=1)) * scale
        q_ref[:, base + NOPE_DIM:base + HEAD_PAD] = rot.astype(BF16)


def _q_up(c_q, w_q, table, tm=512):
    return pl.pallas_call(
        _q_up_kernel,
        grid=(TOKENS // tm,),
        in_specs=[pl.BlockSpec((tm, Q_RANK), lambda i: (i, 0)),
                  pl.BlockSpec((Q_RANK, MLA_HEADS * HEAD_PAD), lambda i: (0, 0)),
                  pl.BlockSpec((tm, 128), lambda i: (i, 0))],
        out_specs=pl.BlockSpec((tm, MLA_HEADS * HEAD_PAD), lambda i: (i, 0)),
        out_shape=jax.ShapeDtypeStruct((TOKENS, MLA_HEADS * HEAD_PAD), BF16),
        compiler_params=_params("parallel"),
        name="q_up",
    )(c_q, w_q, table)


ATT_TQ = 512
ATT_TK = 512


def _attn_kernel(q_ref, k_ref, v_ref, o_ref):
    row = lax.broadcasted_iota(jnp.int32, (ATT_TQ, ATT_TK), 0)
    col = lax.broadcasted_iota(jnp.int32, (ATT_TQ, ATT_TK), 1)
    causal = row >= col
    for qi in range(SEQ // ATT_TQ):
        q = q_ref[qi * ATT_TQ:(qi + 1) * ATT_TQ, :]
        m = l = acc = None
        for kj in range(qi + 1):
            k = k_ref[kj * ATT_TK:(kj + 1) * ATT_TK, :]
            v = v_ref[kj * ATT_TK:(kj + 1) * ATT_TK, :]
            s = lax.dot_general(q, k, (((1,), (1,)), ((), ())), preferred_element_type=F32)
            if kj == qi:
                s = jnp.where(causal, s, -jnp.inf)
            s_max = jnp.max(s, axis=-1, keepdims=True)
            if kj == 0:
                m = s_max
                p = jnp.exp(s - m)
                l = jnp.sum(p, axis=-1, keepdims=True)
                acc = jnp.dot(p.astype(BF16), v, preferred_element_type=F32)
            else:
                m_new = jnp.maximum(m, s_max)
                alpha = jnp.exp(m - m_new)
                p = jnp.exp(s - m_new)
                l = alpha * l + jnp.sum(p, axis=-1, keepdims=True)
                acc = alpha * acc + jnp.dot(p.astype(BF16), v, preferred_element_type=F32)
                m = m_new
        o_ref[qi * ATT_TQ:(qi + 1) * ATT_TQ, :] = (acc / l).astype(o_ref.dtype)


def _attention(q_cat, k_cat, v):
    return pl.pallas_call(
        _attn_kernel,
        grid=(BATCH, MLA_HEADS),
        in_specs=[pl.BlockSpec((SEQ, HEAD_PAD), lambda b, h: (b, h)),
                  pl.BlockSpec((SEQ, HEAD_PAD), lambda b, h: (b, h)),
                  pl.BlockSpec((SEQ, V_DIM), lambda b, h: (b, h))],
        out_specs=pl.BlockSpec((SEQ, V_DIM), lambda b, h: (b, h)),
        out_shape=jax.ShapeDtypeStruct((TOKENS, MLA_HEADS * V_DIM), BF16),
        compiler_params=_params("parallel", "parallel"),
        name="attention",
    )(q_cat, k_cat, v)


def _router_kernel(h_ref, w_hi_ref, w_lo_ref, b_ref, tri_ref, o_ref, cnt_ref, carry_ref):
    @pl.when(pl.program_id(0) == 0)
    def _():
        carry_ref[...] = jnp.zeros_like(carry_ref)

    logits = _dot_split(h_ref[...], w_hi_ref, w_lo_ref) + b_ref[...]
    lane = lax.broadcasted_iota(jnp.int32, logits.shape, 1).astype(F32)
    logits = jnp.where(lane < N_EXPERTS, logits, -jnp.inf)
    m1 = jnp.max(logits, axis=-1, keepdims=True)
    i1 = jnp.min(jnp.where(logits == m1, lane, 128.0), axis=-1, keepdims=True)
    rest = jnp.where(lane == i1, -jnp.inf, logits)
    m2 = jnp.max(rest, axis=-1, keepdims=True)
    i2 = jnp.min(jnp.where(rest == m2, lane, 128.0), axis=-1, keepdims=True)
    e2 = jnp.exp(m2 - m1)
    w1 = 1.0 / (1.0 + e2)
    w2 = e2 / (1.0 + e2)
    first = lane == i1
    second = lane == i2
    chosen = jnp.where(jnp.logical_or(first, second), 1.0, 0.0)
    before = carry_ref[0:1, :] + jnp.dot(tri_ref[...], chosen.astype(BF16),
                                         preferred_element_type=F32)
    r1 = jnp.sum(jnp.where(first, before, 0.0), axis=-1, keepdims=True)
    r2 = jnp.sum(jnp.where(second, before, 0.0), axis=-1, keepdims=True)
    total = carry_ref[0:1, :] + jnp.sum(chosen, axis=0, keepdims=True)
    carry_ref[...] = jnp.broadcast_to(total, carry_ref.shape)
    cnt_ref[...] = jnp.broadcast_to(total, cnt_ref.shape)
    out = jnp.where(lane == 0.0, i1, jnp.where(lane == 1.0, i2, 0.0))
    out = jnp.where(lane == 2.0, w1, jnp.where(lane == 3.0, w2, out))
    out = jnp.where(lane == 4.0, r1, jnp.where(lane == 5.0, r2, out))
    o_ref[...] = out


def _router(h, w_router, b_router, tm=512):
    w_hi, w_lo = _split_weight(jnp.pad(w_router, ((0, 0), (0, 128 - N_EXPERTS))))
    b_pad = jnp.pad(b_router, ((0, 0), (0, 128 - N_EXPERTS)))
    tri = jnp.tril(jnp.ones((tm, tm), BF16), k=-1)
    return pl.pallas_call(
        _router_kernel,
        grid=(TOKENS // tm,),
        in_specs=[pl.BlockSpec((tm, D_MODEL), lambda i: (i, 0)),
                  pl.BlockSpec((D_MODEL, 128), lambda i: (0, 0)),
                  pl.BlockSpec((D_MODEL, 128), lambda i: (0, 0)),
                  pl.BlockSpec((1, 128), lambda i: (0, 0)),
                  pl.BlockSpec((tm, tm), lambda i: (0, 0))],
        out_specs=[pl.BlockSpec((tm, 128), lambda i: (i, 0)),
                   pl.BlockSpec((8, 128), lambda i: (0, 0))],
        out_shape=[jax.ShapeDtypeStruct((TOKENS, 128), F32),
                   jax.ShapeDtypeStruct((8, 128), F32)],
        scratch_shapes=[pltpu.VMEM((8, 128), F32)],
        compiler_params=_params("arbitrary"),
        name="router",
    )(h, w_hi, w_lo, b_pad, tri)


MOE_TM = 1024
MOE_TF = 256
MOE_ROWS = 2 * TOKENS + N_EXPERTS * MOE_TM
MOE_TILES = MOE_ROWS // MOE_TM
MOE_NF = D_FF_EXPERT // MOE_TF
MOE_ISSUE = 96
MOE_BUF_ROWS = MOE_NF * MOE_ISSUE
assert MOE_BUF_ROWS >= MOE_TM
PACKED = D_MODEL // 2


MOE_AHEAD = 2
MOE_SLOTS = MOE_AHEAD + 1


def _moe_kernel(te_ref, nt_ref, src_ref, h_hbm, wg_ref, wu_ref, wd_ref, o_ref,
                xbuf, x_bf, acc_ref, sem):
    i = pl.program_id(0)
    f = pl.program_id(1)
    last = pl.num_programs(1) - 1
    n_valid = nt_ref[0]
    valid = i < n_valid
    slot = lax.rem(i, MOE_SLOTS)
    ahead_slot = lax.rem(i + MOE_AHEAD, MOE_SLOTS)

    def row_copy(tile, r, s):
        tok = src_ref[jnp.minimum(tile, MOE_TILES - 1) * MOE_TM + jnp.minimum(r, MOE_TM - 1)]
        return pltpu.make_async_copy(h_hbm.at[pl.ds(tok, 1), :], xbuf.at[s, pl.ds(r, 1), :],
                                     sem.at[s])

    def wait_rows(s):
        pltpu.make_async_copy(xbuf.at[s], xbuf.at[s], sem.at[s]).wait()

    @pl.when(jnp.logical_and(i == 0, f == 0))
    def _():
        for t in range(MOE_AHEAD):
            def body(r, c, t=t):
                row_copy(t, r, t).start()
                return c
            lax.fori_loop(0, MOE_BUF_ROWS, body, 0, unroll=8)

    @pl.when(jnp.logical_and(f == 0, i < n_valid + MOE_AHEAD))
    def _():
        wait_rows(slot)

    @pl.when(valid)
    def _():
        @pl.when(f == 0)
        def _():
            lo, hi = _unpack_bf16_halves(xbuf[slot, 0:MOE_TM, :])
            x_bf[:, :PACKED] = lo
            x_bf[:, PACKED:] = hi
            acc_ref[...] = jnp.zeros_like(acc_ref)

        for j in range(MOE_ISSUE):
            row_copy(i + MOE_AHEAD, f * MOE_ISSUE + j, ahead_slot).start()

        a = x_bf[...]
        gate = jnp.dot(a, wg_ref[0].astype(BF16), preferred_element_type=F32)
        up = jnp.dot(a, wu_ref[0].astype(BF16), preferred_element_type=F32)
        mid = (_silu(gate) * up).astype(BF16)
        acc_ref[...] += jnp.dot(mid, wd_ref[0].astype(BF16), preferred_element_type=F32)

        @pl.when(f == last)
        def _():
            o_ref[...] = acc_ref[...].astype(o_ref.dtype)

        @pl.when(jnp.logical_and(f == last, i >= MOE_TILES - MOE_AHEAD))
        def _():
            wait_rows(ahead_slot)

    @pl.when(jnp.logical_and(jnp.logical_not(valid), f == last))
    def _():
        o_ref[...] = jnp.zeros_like(o_ref)


def _moe_experts(tile_expert, n_tiles, src, h_packed, wg, wu, wd):
    def f_of(i, f, nt):
        return jnp.where(i < nt[0], f, MOE_NF - 1)

    grid_spec = pltpu.PrefetchScalarGridSpec(
        num_scalar_prefetch=3,
        grid=(MOE_TILES, MOE_NF),
        in_specs=[
            pl.BlockSpec(memory_space=pl.ANY),
            pl.BlockSpec((1, D_MODEL, MOE_TF),
                         lambda i, f, te, nt, src: (te[i], 0, f_of(i, f, nt))),
            pl.BlockSpec((1, D_MODEL, MOE_TF),
                         lambda i, f, te, nt, src: (te[i], 0, f_of(i, f, nt))),
            pl.BlockSpec((1, MOE_TF, D_MODEL),
                         lambda i, f, te, nt, src: (te[i], f_of(i, f, nt), 0)),
        ],
        out_specs=pl.BlockSpec((MOE_TM, D_MODEL), lambda i, f, te, nt, src: (i, 0)),
        scratch_shapes=[pltpu.VMEM((MOE_SLOTS, MOE_BUF_ROWS, PACKED), U32),
                        pltpu.VMEM((MOE_TM, D_MODEL), BF16),
                        pltpu.VMEM((MOE_TM, D_MODEL), F32),
                        pltpu.SemaphoreType.DMA((MOE_SLOTS,))],
    )
    return pl.pallas_call(
        _moe_kernel,
        grid_spec=grid_spec,
        out_shape=jax.ShapeDtypeStruct((MOE_ROWS, D_MODEL), BF16),
        compiler_params=_params("arbitrary", "arbitrary"),
        name="moe_experts",
    )(tile_expert, n_tiles, src, h_packed, wg, wu, wd)


def _combine_ln_kernel(h_ref, y0_ref, y1_ref, w0_ref, w1_ref, g_ref, b_ref, p_ref, wpg_ref,
                       wpp_ref, o_ref):
    ff = w0_ref[...] * y0_ref[...].astype(F32) + w1_ref[...] * y1_ref[...].astype(F32)
    h2 = _layer_norm(DEEPNORM_ALPHA * h_ref[...] + ff, g_ref[...], b_ref[...])
    o_ref[...] = _add_ple(h2, p_ref, wpg_ref, wpp_ref)


def _combine_ln_ple(h, y0, y1, w0, w1, g, b, p_all, layer, wpg, wpp, tm=512):
    row = pl.BlockSpec((tm, D_MODEL), lambda i: (i, 0))
    col = pl.BlockSpec((tm, 1), lambda i: (i, 0))
    return pl.pallas_call(
        _combine_ln_kernel,
        grid=(TOKENS // tm,),
        in_specs=[row, row, row, col, col, _resident((1, D_MODEL)), _resident((1, D_MODEL)),
                  pl.BlockSpec((None, tm, PLE_DIM), lambda i: (layer, i, 0)),
                  _resident((D_MODEL, D_MODEL)), _resident((PLE_DIM, D_MODEL))],
        out_specs=row,
        out_shape=jax.ShapeDtypeStruct((TOKENS, D_MODEL), F32),
        compiler_params=_params("parallel"),
        name="combine_ln_ple",
    )(h, y0, y1, w0, w1, g, b, p_all, wpg, wpp)


def _swap_halves(w):
    half = w.shape[-1] // 2
    return jnp.concatenate([w[..., half:], w[..., :half]], axis=-1)


def kernel(x, p, positions, ssm_w_in, ssm_conv_w, ssm_conv_b, ssm_dt_bias, ssm_a_log, ssm_d, ssm_norm_g, ssm_w_out, kv_w_down, kv_norm_g, kv_w_rope, kv_w_uk, kv_w_uv, mla_w_dq, mla_q_norm_g, mla_w_uq, mla_w_o, ffn_w_gate, ffn_w_up, ffn_w_down, moe_w_router, moe_b_router, moe_w_gate, moe_w_up, moe_w_down, ln1_g, ln1_b, ln2_g, ln2_b, ple_w_proj, ple_w_gate):
    h0 = x.reshape(TOKENS, D_MODEL)
    p = p.reshape(DEPTH, TOKENS, PLE_DIM)
    row = lambda v: v.reshape(1, -1)

    zx_cols = D_INNER + CONV_DIM
    w_dt = ssm_w_in[0][:, zx_cols:]
    zx = _matmul(h0, ssm_w_in[0].astype(BF16), zx_cols, BF16, tm=1024, tn=1024)
    parts, acum = _dt_prep(h0, w_dt, row(ssm_dt_bias[0]), row(ssm_a_log[0]))
    xbc = _conv_silu(zx, ssm_conv_w[0], row(ssm_conv_b[0]))
    parts = parts.reshape(N_WIDE * N_SPLIT, BATCH, SEQ, SSM_GROUPS, HEADS_PER_GROUP)
    parts = parts.transpose(1, 3, 2, 0, 4).reshape(BATCH, SSM_GROUPS, SEQ, WIDE_ROWS)
    acumt = acum.reshape(BATCH, N_CHUNKS, SSM_CHUNK, SSM_GROUPS, HEADS_PER_GROUP)
    acumt = acumt.transpose(0, 3, 1, 4, 2)
    d_full = row(jnp.repeat(ssm_d[0], SSM_HEAD_DIM))
    y = _ssd(xbc, zx, parts, acumt, d_full, row(ssm_norm_g[0]))
    (h,) = _matmul_ln(y, ssm_w_out[0].astype(BF16), h0, row(ln1_g[0]), row(ln1_b[0]),
                      also_packed=False)

    h = _swiglu_ln_ple(h, ffn_w_gate[0].astype(BF16), ffn_w_up[0].astype(BF16),
                       ffn_w_down[0].astype(BF16), row(ln2_g[0]), row(ln2_b[0]),
                       p[0], ple_w_gate[0].astype(BF16), ple_w_proj[0].astype(BF16))

    inv_freq = ROPE_THETA ** (-jnp.arange(0, ROPE_DIM, 2, dtype=F32) / ROPE_DIM)
    table = _rope_table(positions.reshape(TOKENS, 1), row(jnp.tile(inv_freq, 4)))
    w_lat = jnp.concatenate([kv_w_down, mla_w_dq[0], kv_w_rope, _swap_halves(kv_w_rope)],
                            axis=1).astype(BF16)
    c_kv, c_q, k_rope = _latent(h, w_lat, row(kv_norm_g), row(mla_q_norm_g[0]), table)
    k_cat, v = _kv_up(c_kv, kv_w_uk.astype(BF16), kv_w_uv.astype(BF16), k_rope)
    w_uq = mla_w_uq[0].reshape(Q_RANK, MLA_HEADS, NOPE_DIM + ROPE_DIM)
    w_q = jnp.concatenate([w_uq, _swap_halves(w_uq[..., NOPE_DIM:])], axis=-1)
    w_q = w_q.reshape(Q_RANK, MLA_HEADS * HEAD_PAD).astype(BF16)
    q_cat = _q_up(c_q, w_q, table)
    o = _attention(q_cat, k_cat, v)
    h, h_packed = _matmul_ln(o, mla_w_o[0].astype(BF16), h, row(ln1_g[1]), row(ln1_b[1]),
                             also_packed=True)

    route, counts = _router(h, moe_w_router[0], row(moe_b_router[0]))
    counts = counts[0, :N_EXPERTS].astype(jnp.int32)
    tiles_per = (counts + MOE_TM - 1) // MOE_TM
    tile_end = jnp.cumsum(tiles_per)
    start = (tile_end - tiles_per) * MOE_TM
    expert = route[:, 0:2].astype(jnp.int32)
    rank = route[:, 4:6].astype(jnp.int32)
    start_of = jnp.sum(jnp.where(expert[..., None] == jnp.arange(N_EXPERTS, dtype=jnp.int32),
                                 start, 0), axis=-1)
    pos = start_of + rank
    n_tiles = tile_end[-1:].astype(jnp.int32)
    tile_ids = jnp.minimum(jnp.arange(MOE_TILES, dtype=jnp.int32), n_tiles[0] - 1)
    tile_expert = jnp.searchsorted(tile_end, tile_ids, side="right").astype(jnp.int32)
    token = jnp.broadcast_to(jnp.arange(TOKENS, dtype=jnp.int32)[:, None], (TOKENS, 2))
    src = jnp.zeros((MOE_ROWS,), jnp.int32).at[pos.reshape(-1)].set(
        token.reshape(-1), unique_indices=True)
    y_sorted = _moe_experts(tile_expert, n_tiles, src, h_packed, moe_w_gate[0], moe_w_up[0],
                            moe_w_down[0])
    y0 = y_sorted.at[pos[:, 0]].get(mode="promise_in_bounds")
    y1 = y_sorted.at[pos[:, 1]].get(mode="promise_in_bounds")
    h = _combine_ln_ple(h, y0, y1, route[:, 2:3], route[:, 3:4], row(ln2_g[1]), row(ln2_b[1]),
                        p, 1, ple_w_gate[1].astype(BF16), ple_w_proj[1].astype(BF16))
    return h.reshape(BATCH, SEQ, D_MODEL)
```
